```python
import math
import jax, jax.numpy as jnp
from jax import lax
import numpy as np


D_MODEL = 1024
BATCH = 2
SEQ = 8192
DEPTH = 4

GRID_W = 64
CTX_LEN = 256
EPS = 1e-6

DN_H = 4
DN_DK = 128
DN_DV = 128
SHORT_CONV = 3
GLA_H = 4
GLA_DK = 64
GLA_DV = 128
GLA_RANK = 16
GLA_TAU = 16.0
CHUNK = 64
REC_SIZES = (DN_H * DN_DK, DN_H * DN_DK, DN_H * DN_DV, DN_H * DN_DV, 2 * DN_H, 2 * DN_H,
             GLA_H * GLA_DK, GLA_H * GLA_DK, GLA_H * GLA_DV, GLA_H * GLA_DV, 2 * GLA_RANK)
REC_IN = sum(REC_SIZES)
REC_MIX = DN_H * DN_DV + GLA_H * GLA_DV

ATT_H = 8
ATT_KVH = 2
ATT_G = ATT_H // ATT_KVH
ATT_HD = 128
ATT_QKV = (ATT_H + 2 * ATT_KVH) * ATT_HD
Q_BLOCK = 128
ROPE_THETA = 10000.0
ROPE_PAIRS = ATT_HD // 4

D_FF = 2816
FFN_CONV = 3

kernel_name = 'hybrid_deltanet_gla_gqa_dit'


def _rms(x):
    xf = x.astype(jnp.float32)
    return (xf * lax.rsqrt(jnp.mean(xf * xf, axis=-1, keepdims=True) + EPS)).astype(x.dtype)


def _l2n(x):
    return x * lax.rsqrt(jnp.sum(x * x, axis=-1, keepdims=True) + EPS)


def _modulate(h, shift, scale):
    return h * (1.0 + scale) + shift


def _ada(cond, w, b):
    m = (jax.nn.silu(cond) @ w + b)[:, None, :]
    return jnp.split(m, 6, axis=-1)


def _dwconv(x, w):
    K = w.shape[0]
    p = K // 2
    T = x.shape[1]
    xp = jnp.pad(x, ((0, 0), (p, p), (0, 0)))
    out = xp[:, 0:T] * w[0]
    for i in range(1, K):
        out = out + xp[:, i:i + T] * w[i]
    return out


def _heads(a, n_heads):
    B, T, _ = a.shape
    return jnp.swapaxes(a.reshape(B, T, n_heads, -1), 1, 2)


def _tokens(a):
    B, H, T, hd = a.shape
    return jnp.swapaxes(a, 1, 2).reshape(B, T, H * hd)


def _identity(a):
    return a


def _reverse(a):
    return jnp.flip(a, axis=2)


def _delta_chunked(q, k, v, g, beta, s0):
    B, H, T, _ = q.shape
    dv = v.shape[-1]
    n = T // CHUNK
    q, k, v = (a.reshape(B, H, n, CHUNK, a.shape[-1]) for a in (q, k, v))
    g, beta = (a.reshape(B, H, n, CHUNK) for a in (g, beta))
    G = jnp.cumsum(g, axis=-1)
    incl = jnp.tril(jnp.ones((CHUNK, CHUNK), bool))
    decay = jnp.exp(jnp.where(incl, G[..., :, None] - G[..., None, :], -jnp.inf))
    kb = k * beta[..., None]
    A = jnp.tril(jnp.einsum('bhnid,bhnjd->bhnij', kb, k) * decay, -1)
    eye = jnp.eye(CHUNK, dtype=q.dtype)
    tinv = lax.linalg.triangular_solve(eye + A, jnp.broadcast_to(eye, A.shape),
                                       left_side=True, lower=True, unit_diagonal=True)
    u = tinv @ (v * beta[..., None])
    w = tinv @ (kb * jnp.exp(G)[..., None])
    intra = jnp.einsum('bhnid,bhnjd->bhnij', q, k) * decay
    q_dec = q * jnp.exp(G)[..., None]
    k_dec = k * jnp.exp(G[..., -1:] - G)[..., None]
    g_last = jnp.exp(G[..., -1])

    def step(S, xs):
        qd, kd, w_c, u_c, a_c, gl = xs
        v_new = u_c - w_c @ S
        o = qd @ S + a_c @ v_new
        return S * gl[..., None, None] + jnp.swapaxes(kd, -1, -2) @ v_new, o

    xs = tuple(jnp.moveaxis(a, 2, 0) for a in (q_dec, k_dec, w, u, intra, g_last))
    S, o = lax.scan(step, s0, xs)
    return jnp.moveaxis(o, 0, 2).reshape(B, H, T, dv), S


def _gla_chunked(q, k, v, log_a, s0):
    B, H, T, _ = q.shape
    dv = v.shape[-1]
    n = T // CHUNK
    q, k, v, log_a = (a.reshape(B, H, n, CHUNK, a.shape[-1]) for a in (q, k, v, log_a))
    b = jnp.cumsum(log_a, axis=3)
    b_mid = b[:, :, :, CHUNK // 2:CHUNK // 2 + 1]
    att = jnp.tril(jnp.einsum('bhnid,bhnjd->bhnij', q * jnp.exp(b - b_mid), k * jnp.exp(b_mid - b)))
    o_intra = att @ v
    q_inter = q * jnp.exp(b)
    k_state = k * jnp.exp(b[..., -1:, :] - b)
    a_last = jnp.exp(b[..., -1, :])

    def step(S, xs):
        qd, kd, vc, al = xs
        o = qd @ S
        return S * al[..., :, None] + jnp.swapaxes(kd, -1, -2) @ vc, o

    xs = tuple(jnp.moveaxis(a, 2, 0) for a in (q_inter, k_state, v, a_last))
    S, o_inter = lax.scan(step, s0, xs)
    o = o_intra + jnp.moveaxis(o_inter, 0, 2)
    return o.reshape(B, H, T, dv), S


def _bidir_scan(scan_fn, shared_c, dir_c, shared_l, dir_l, state_shape):
    out_c = 0.0
    out_l = 0.0
    for d in range(2):
        f = _identity if d == 0 else _reverse
        args_c = [f(a) for a in shared_c] + [f(a[d]) for a in dir_c]
        args_l = [f(a) for a in shared_l] + [f(a[d]) for a in dir_l]
        o_c, s_c = scan_fn(*args_c, jnp.zeros(state_shape, jnp.float32))
        o_l, _ = scan_fn(*args_l, s_c)
        out_c = out_c + f(o_c)
        out_l = out_l + f(o_l)
    return out_c, out_l


def _recurrent_mixer(h_c, h_l, w_in, conv_w, a_log, dt_bias, dn_norm, gla_w2, gla_b2, gla_norm, w_out):
    offs = np.cumsum(REC_SIZES)[:-1].tolist()
    f32 = jnp.float32

    def prep(h):
        B, T, _ = h.shape
        dq, dk, dv, dz, da, db, gq, gk, gv, gr, gg = jnp.split(h @ w_in, offs, axis=-1)
        qkv = jax.nn.silu(_dwconv(jnp.concatenate([dq, dk, dv], axis=-1), conv_w)).astype(f32)
        dq, dk, dv = jnp.split(qkv, [DN_H * DN_DK, 2 * DN_H * DN_DK], axis=-1)
        dq = _l2n(_heads(dq, DN_H)) * DN_DK ** -0.5
        dk = _l2n(_heads(dk, DN_H))
        dv = _heads(dv, DN_H)
        g = -jnp.exp(a_log) * jax.nn.softplus(da.astype(f32).reshape(B, T, 2, DN_H) + dt_bias)
        beta = jax.nn.sigmoid(db.astype(f32).reshape(B, T, 2, DN_H))
        g = jnp.transpose(g, (2, 0, 3, 1))
        beta = jnp.transpose(beta, (2, 0, 3, 1))
        gq = _heads(gq.astype(f32), GLA_H) * GLA_DK ** -0.5
        gk = _heads(gk.astype(f32), GLA_H)
        gv = _heads(gv.astype(f32), GLA_H)
        la = jnp.einsum('btdr,drk->dbtk', gg.astype(f32).reshape(B, T, 2, GLA_RANK), gla_w2) + gla_b2[:, None, None, :]
        la = jax.nn.log_sigmoid(la) / GLA_TAU
        la = jnp.transpose(la.reshape(2, B, T, GLA_H, GLA_DK), (0, 1, 3, 2, 4))
        return (dq, dk, dv), (g, beta), (gq, gk, gv), (la,), _heads(dz, DN_H), _heads(gr, GLA_H)

    dn_sc, dn_dc, gla_sc, gla_dc, z_c, r_c = prep(h_c)
    dn_sl, dn_dl, gla_sl, gla_dl, z_l, r_l = prep(h_l)
    B = h_l.shape[0]
    dn_c, dn_l = _bidir_scan(_delta_chunked, dn_sc, dn_dc, dn_sl, dn_dl, (B, DN_H, DN_DK, DN_DV))
    gla_c, gla_l = _bidir_scan(_gla_chunked, gla_sc, gla_dc, gla_sl, gla_dl, (B, GLA_H, GLA_DK, GLA_DV))

    def merge(dn, gla, z, r, dtype):
        dn = _rms(dn) * dn_norm * jax.nn.silu(z)
        gla = _rms(gla) * gla_norm * jax.nn.silu(r)
        return jnp.concatenate([_tokens(dn), _tokens(gla)], axis=-1).astype(dtype) @ w_out

    return merge(dn_c, gla_c, z_c, r_c, h_c.dtype), merge(dn_l, gla_l, z_l, r_l, h_l.dtype)


def _rope_half(x, ang):
    x1, x2 = jnp.split(x.astype(jnp.float32), 2, axis=-1)
    cos = jnp.cos(ang)[:, None, :]
    sin = jnp.sin(ang)[:, None, :]
    return jnp.concatenate([x1 * cos - x2 * sin, x1 * sin + x2 * cos], axis=-1)


def _rope2d(x, ang_row, ang_col):
    xr, xc = jnp.split(x, 2, axis=-1)
    return jnp.concatenate([_rope_half(xr, ang_row), _rope_half(xc, ang_col)], axis=-1).astype(x.dtype)


def _attend(q, k, v):
    s = jnp.einsum('bqkgd,bskd->bkgqs', q.astype(jnp.float32), k.astype(jnp.float32)) * ATT_HD ** -0.5
    p = jax.nn.softmax(s, axis=-1)
    return jnp.einsum('bkgqs,bskd->bqkgd', p.astype(v.dtype), v)


def _attention_mixer(h_c, h_l, w_qkv, q_norm, k_norm, w_out, ang_row, ang_col, need_ctx):
    def prep(h, rotary):
        B, T, _ = h.shape
        q, k, v = jnp.split(h @ w_qkv, [ATT_H * ATT_HD, (ATT_H + ATT_KVH) * ATT_HD], axis=-1)
        q = _rms(q.reshape(B, T, ATT_H, ATT_HD)) * q_norm
        k = _rms(k.reshape(B, T, ATT_KVH, ATT_HD)) * k_norm
        if rotary:
            q = _rope2d(q, ang_row, ang_col)
            k = _rope2d(k, ang_row, ang_col)
        return q.reshape(B, T, ATT_KVH, ATT_G, ATT_HD), k, v.reshape(B, T, ATT_KVH, ATT_HD)

    qc, kc, vc = prep(h_c, False)
    ql, kl, vl = prep(h_l, True)
    keys = jnp.concatenate([kc, kl], axis=1)
    vals = jnp.concatenate([vc, vl], axis=1)
    B, T, _ = h_l.shape
    nb = T // Q_BLOCK
    qb = jnp.moveaxis(ql.reshape(B, nb, Q_BLOCK, ATT_KVH, ATT_G, ATT_HD), 1, 0)
    ol = lax.map(lambda qblk: _attend(qblk, keys, vals), qb)
    ol = jnp.moveaxis(ol, 0, 1).reshape(B, T, ATT_H * ATT_HD) @ w_out
    if need_ctx:
        oc = _attend(qc, kc, vc).reshape(B, h_c.shape[1], ATT_H * ATT_HD) @ w_out
    else:
        oc = None
    return oc, ol


def _conv_ffn(h, w_up, conv_w, w_down):
    gate, val = jnp.split(h @ w_up, 2, axis=-1)
    gate = _dwconv(gate, conv_w)
    return (jax.nn.silu(gate) * val) @ w_down


def setup_inputs(seed: int = 0) -> dict:
    key = jax.random.key(seed)
    ks = iter(jax.random.split(key, 32))
    f32 = jnp.float32

    def nrm(shape, scale):
        return jax.random.normal(next(ks), shape, f32) * scale

    def gain(shape):
        return 1.0 + nrm(shape, 0.02)

    NE = (DEPTH + 1) // 2
    NO = DEPTH // 2
    dt = jnp.exp(jax.random.uniform(next(ks), (NE, 2, DN_H), f32, math.log(1e-3), math.log(1e-1)))
    return {
        'x': nrm((BATCH, SEQ, D_MODEL), 1.0),
        'c': nrm((BATCH, D_MODEL), 1.0),
        'ctx': nrm((BATCH, CTX_LEN, D_MODEL), 1.0),
        'c_ctx': nrm((D_MODEL,), 1.0),
        'mod_w': nrm((DEPTH, D_MODEL, 6 * D_MODEL), 0.5 * D_MODEL ** -0.5),
        'mod_b': nrm((DEPTH, 6 * D_MODEL), 0.02),
        'rec_w_in': nrm((NE, D_MODEL, REC_IN), D_MODEL ** -0.5),
        'rec_conv': nrm((NE, SHORT_CONV, 2 * DN_H * DN_DK + DN_H * DN_DV), 0.5),
        'dn_a_log': jnp.log(jax.random.uniform(next(ks), (NE, 2, DN_H), f32, 1.0, 16.0)),
        'dn_dt_bias': dt + jnp.log(-jnp.expm1(-dt)),
        'dn_norm': gain((NE, DN_DV)),
        'gla_w2': nrm((NE, 2, GLA_RANK, GLA_H * GLA_DK), GLA_RANK ** -0.5),
        'gla_b2': nrm((NE, 2, GLA_H * GLA_DK), 0.1),
        'gla_norm': gain((NE, GLA_DV)),
        'rec_w_out': nrm((NE, REC_MIX, D_MODEL), REC_MIX ** -0.5),
        'att_w_qkv': nrm((NO, D_MODEL, ATT_QKV), D_MODEL ** -0.5),
        'att_q_norm': gain((NO, ATT_HD)),
        'att_k_norm': gain((NO, ATT_HD)),
        'att_w_out': nrm((NO, ATT_H * ATT_HD, D_MODEL), (ATT_H * ATT_HD) ** -0.5),
        'ffn_w_up': nrm((DEPTH, D_MODEL, 2 * D_FF), D_MODEL ** -0.5),
        'ffn_conv': nrm((DEPTH, FFN_CONV, D_FF), 0.5),
        'ffn_w_down': nrm((DEPTH, D_FF, D_MODEL), D_FF ** -0.5),
        'final_norm': gain((D_MODEL,)),
    }


def reference(x, c, ctx, c_ctx, mod_w, mod_b, rec_w_in, rec_conv, dn_a_log, dn_dt_bias, dn_norm,
              gla_w2, gla_b2, gla_norm, rec_w_out, att_w_qkv, att_q_norm, att_k_norm, att_w_out,
              ffn_w_up, ffn_conv, ffn_w_down, final_norm):
    f32 = jnp.float32
    n_lat = x.shape[1]
    ROWS = n_lat // GRID_W
    row = jnp.repeat(jnp.arange(ROWS, dtype=f32), GRID_W)
    col = jnp.tile(jnp.arange(GRID_W, dtype=f32), ROWS)
    inv_freq = ROPE_THETA ** (-jnp.arange(ROPE_PAIRS, dtype=f32) / ROPE_PAIRS)
    ang_row = row[:, None] * inv_freq
    ang_col = col[:, None] * inv_freq

    for i in range(DEPTH):
        last = i == DEPTH - 1
        ml = _ada(c, mod_w[i], mod_b[i])
        mc = _ada(c_ctx[None], mod_w[i], mod_b[i])
        h_l = _modulate(_rms(x), ml[0], ml[1])
        h_c = _modulate(_rms(ctx), mc[0], mc[1])
        if i % 2 == 0:
            e = i // 2
            mix_c, mix_l = _recurrent_mixer(h_c, h_l, rec_w_in[e], rec_conv[e], dn_a_log[e], dn_dt_bias[e],
                                            dn_norm[e], gla_w2[e], gla_b2[e], gla_norm[e], rec_w_out[e])
        else:
            o = i // 2
            mix_c, mix_l = _attention_mixer(h_c, h_l, att_w_qkv[o], att_q_norm[o], att_k_norm[o], att_w_out[o],
                                            ang_row, ang_col, not last)
        x = x + ml[2] * mix_l
        x = x + ml[5] * _conv_ffn(_modulate(_rms(x), ml[3], ml[4]), ffn_w_up[i], ffn_conv[i], ffn_w_down[i])
        if not last:
            ctx = ctx + mc[2] * mix_c
            ctx = ctx + mc[5] * _conv_ffn(_modulate(_rms(ctx), mc[3], mc[4]), ffn_w_up[i], ffn_conv[i], ffn_w_down[i])

    return _rms(x) * final_norm
```

```python
import functools
import math

import jax
import jax.numpy as jnp
from jax import lax
from jax.experimental import pallas as pl
from jax.experimental.pallas import tpu as pltpu

f32 = jnp.float32
bf16 = jnp.bfloat16

EPS = 1e-6
GRID_W = 64
DN_H = 4
DN_DK = 128
DN_DV = 128
GLA_H = 4
GLA_DK = 64
GLA_DV = 128
GLA_RANK = 16
GLA_TAU = 16.0
CHUNK = 64
ATT_H = 8
ATT_KVH = 2
ATT_G = ATT_H // ATT_KVH
ATT_HD = 128
ROPE_THETA = 10000.0
ROPE_PAIRS = ATT_HD // 4
FFN_CHUNK = 256
MOD_ROWS = 8
LANES = 128
VMEM_LIMIT = 56 * 1024 * 1024


def _cparams(*sem):
    return pltpu.CompilerParams(dimension_semantics=sem, vmem_limit_bytes=VMEM_LIMIT)


def _const_spec(shape):
    nd = len(shape)
    return pl.BlockSpec(shape, lambda *_: (0,) * nd, pipeline_mode=pl.Buffered(1))


class _Layout:
    def __init__(self, B, T, Tc):
        self.B, self.T, self.Tc = B, T, Tc
        self.R = B * Tc + B * T
        self.tm = min(256, Tc)
        assert Tc % self.tm == 0 and T % self.tm == 0 and T % CHUNK == 0 and Tc % CHUNK == 0

    def mod_row(self, row0):
        return jnp.where(row0 < self.B * self.Tc, self.B, (row0 - self.B * self.Tc) // self.T)

    def is_seq_start(self, row0):
        return jnp.where(row0 < self.B * self.Tc, row0 % self.Tc == 0, (row0 - self.B * self.Tc) % self.T == 0)


def _normmod(x, m, shift_i, scale_i):
    ms = jnp.mean(x * x, axis=-1, keepdims=True)
    return (x * lax.rsqrt(ms + EPS)) * (1.0 + m[scale_i:scale_i + 1, :]) + m[shift_i:shift_i + 1, :]


def _silu(x):
    return x * jax.nn.sigmoid(x)


def _split2(x):
    hi = x.astype(bf16)
    lo = (x - hi.astype(f32)).astype(bf16)
    return hi, lo


def _dot(a, b):
    return jnp.dot(a, b, preferred_element_type=f32)


def _dot_nt(a, b):
    return lax.dot_general(a, b, (((1,), (1,)), ((), ())), preferred_element_type=f32)


def _dot_tn(a, b):
    return lax.dot_general(a, b, (((0,), (0,)), ((), ())), preferred_element_type=f32)


def _mod_body(c_ref, w_ref, b_ref, o_ref):
    a = _silu(c_ref[...]).astype(bf16)
    o_ref[0] = _dot(a, w_ref[0].astype(bf16)) + b_ref[0]


def _mods(cond, mod_w, mod_b):
    depth, D, D6 = mod_w.shape
    tn = 1536
    out = pl.pallas_call(
        _mod_body,
        out_shape=jax.ShapeDtypeStruct((depth, MOD_ROWS, D6), f32),
        grid=(depth, D6 // tn),
        in_specs=[pl.BlockSpec((MOD_ROWS, D), lambda l, j: (0, 0)),
                  pl.BlockSpec((1, D, tn), lambda l, j: (l, 0, j)),
                  pl.BlockSpec((1, 1, tn), lambda l, j: (l, 0, j))],
        out_specs=pl.BlockSpec((1, MOD_ROWS, tn), lambda l, j: (l, 0, j)),
        compiler_params=_cparams("parallel", "parallel"),
        name="mods",
    )(cond, mod_w, mod_b.reshape(depth, 1, D6))
    return out.reshape(depth, MOD_ROWS, 6, D)


def _ffn_body(xp_ref, x_ref, xn_ref, m_ref, wg_ref, wv_ref, cw_ref, wd_ref, o_ref, h_sc, g_sc, acc_sc, *, lay, halo, ncf):
    tm = lay.tm
    row0 = pl.program_id(0) * tm
    m = m_ref[0]
    x = x_ref[...]
    xe = jnp.concatenate([xp_ref[...], x, xn_ref[...]], axis=0)
    h_sc[...] = _normmod(xe, m, 3, 4).astype(bf16)
    rowi = lax.broadcasted_iota(jnp.int32, (tm, 1), 0)
    kill_prev = jnp.logical_and(rowi == 0, lay.is_seq_start(row0))
    kill_next = jnp.logical_and(rowi == tm - 1, jnp.logical_or(lay.is_seq_start(row0 + tm), row0 + tm == lay.R))
    acc_sc[...] = jnp.zeros_like(acc_sc)

    def step(c, carry):
        g_sc[...] = _dot(h_sc[...], wg_ref[c])
        val = _dot(h_sc[pl.ds(halo, tm), :], wv_ref[c])
        cw = cw_ref[c]
        gm1 = jnp.where(kill_prev, 0.0, g_sc[pl.ds(halo - 1, tm), :])
        g0 = g_sc[pl.ds(halo, tm), :]
        gp1 = jnp.where(kill_next, 0.0, g_sc[pl.ds(halo + 1, tm), :])
        gate = gm1 * cw[0:1, :] + g0 * cw[1:2, :] + gp1 * cw[2:3, :]
        act = (_silu(gate) * val).astype(bf16)
        acc_sc[...] += _dot(act, wd_ref[c])
        return carry

    lax.fori_loop(0, ncf, step, 0)
    o_ref[...] = x + m[5:6, :] * acc_sc[...]


def _ffn(xs, mods_l, wg, wv, cw, wd, lay):
    R, D = xs.shape
    tm = lay.tm
    halo = 16
    ncf = wg.shape[0]
    cf = wg.shape[2]
    nh = tm // halo
    nhb = R // halo
    body = functools.partial(_ffn_body, lay=lay, halo=halo, ncf=ncf)
    return pl.pallas_call(
        body,
        out_shape=jax.ShapeDtypeStruct((R, D), f32),
        grid=(R // tm,),
        in_specs=[pl.BlockSpec((halo, D), lambda i: (jnp.maximum(i * nh - 1, 0), 0)),
                  pl.BlockSpec((tm, D), lambda i: (i, 0)),
                  pl.BlockSpec((halo, D), lambda i: (jnp.minimum((i + 1) * nh, nhb - 1), 0)),
                  pl.BlockSpec((1, 6, D), lambda i: (lay.mod_row(i * tm), 0, 0)),
                  _const_spec(wg.shape), _const_spec(wv.shape), _const_spec(cw.shape), _const_spec(wd.shape)],
        out_specs=pl.BlockSpec((tm, D), lambda i: (i, 0)),
        scratch_shapes=[pltpu.VMEM((tm + 2 * halo, D), bf16),
                        pltpu.VMEM((tm + 2 * halo, cf), f32),
                        pltpu.VMEM((tm, D), f32)],
        compiler_params=_cparams("parallel"),
        name="conv_ffn",
    )(xs, xs, xs, mods_l, wg, wv, cw, wd)


def _qkv_body(x_ref, m_ref, w_ref, qn_ref, kn_ref, cos_ref, sin_ref, q_ref, k_ref, v_ref):
    h = _normmod(x_ref[...], m_ref[0], 0, 1).astype(bf16)
    y = _dot(h, w_ref[...])
    cos = cos_ref[...]
    sin = sin_ref[...]
    lane = lax.broadcasted_iota(jnp.int32, cos.shape, 1)
    first = (lane % (2 * ROPE_PAIRS)) < ROPE_PAIRS

    def norm_rope(z, gain):
        z = z * lax.rsqrt(jnp.mean(z * z, axis=-1, keepdims=True) + EPS) * gain
        partner = jnp.where(first, pltpu.roll(z, LANES - ROPE_PAIRS, axis=1), pltpu.roll(z, ROPE_PAIRS, axis=1))
        return z * cos + partner * sin

    nq = ATT_H * ATT_HD
    for hh in range(ATT_H):
        z = norm_rope(y[:, hh * ATT_HD:(hh + 1) * ATT_HD], qn_ref[...])
        q_ref[:, hh * ATT_HD:(hh + 1) * ATT_HD] = (z * (ATT_HD ** -0.5)).astype(bf16)
    for kk in range(ATT_KVH):
        z = norm_rope(y[:, nq + kk * ATT_HD:nq + (kk + 1) * ATT_HD], kn_ref[...])
        k_ref[:, kk * ATT_HD:(kk + 1) * ATT_HD] = z.astype(bf16)
    nk = nq + ATT_KVH * ATT_HD
    v_ref[...] = y[:, nk:nk + ATT_KVH * ATT_HD].astype(bf16)


def _qkv(xs, mods_l, w, qn, kn, cos, sin, lay):
    R, D = xs.shape
    tm = lay.tm
    nq = ATT_H * ATT_HD
    nkv = ATT_KVH * ATT_HD
    return pl.pallas_call(
        _qkv_body,
        out_shape=(jax.ShapeDtypeStruct((R, nq), bf16), jax.ShapeDtypeStruct((R, nkv), bf16),
                   jax.ShapeDtypeStruct((R, nkv), bf16)),
        grid=(R // tm,),
        in_specs=[pl.BlockSpec((tm, D), lambda i: (i, 0)),
                  pl.BlockSpec((1, 6, D), lambda i: (lay.mod_row(i * tm), 0, 0)),
                  _const_spec(w.shape), _const_spec(qn.shape), _const_spec(kn.shape),
                  pl.BlockSpec((tm, ATT_HD), lambda i: (i, 0)),
                  pl.BlockSpec((tm, ATT_HD), lambda i: (i, 0))],
        out_specs=(pl.BlockSpec((tm, nq), lambda i: (i, 0)), pl.BlockSpec((tm, nkv), lambda i: (i, 0)),
                   pl.BlockSpec((tm, nkv), lambda i: (i, 0))),
        compiler_params=_cparams("parallel"),
        name="qkv_rope",
    )(xs, mods_l, w, qn, kn, cos, sin)


def _flash_body(q_ref, k_ref, v_ref, o_ref, m_sc, l_sc, acc_sc, *, tq, ncq, nck):
    qi = pl.program_id(2)
    j = pl.program_id(3)

    @pl.when(j == 0)
    def _():
        m_sc[...] = jnp.full_like(m_sc, -jnp.inf)
        l_sc[...] = jnp.zeros_like(l_sc)
        acc_sc[...] = jnp.zeros_like(acc_sc)

    @pl.when(jnp.logical_or(qi >= ncq, j < nck))
    def _():
        q = q_ref[...]
        q4 = jnp.concatenate([q[:, g * ATT_HD:(g + 1) * ATT_HD] for g in range(ATT_G)], axis=0)
        s = _dot_nt(q4, k_ref[...])
        m_prev = m_sc[...]
        m_new = jnp.maximum(m_prev, jnp.max(s, axis=-1, keepdims=True))
        p = jnp.exp(s - m_new)
        alpha = jnp.exp(m_prev - m_new)
        l_sc[...] = alpha * l_sc[...] + jnp.sum(p, axis=-1, keepdims=True)
        acc_sc[...] = alpha * acc_sc[...] + _dot(p.astype(bf16), v_ref[...])
        m_sc[...] = m_new

    @pl.when(j == pl.num_programs(3) - 1)
    def _():
        out = acc_sc[...] / l_sc[...]
        for g in range(ATT_G):
            o_ref[:, g * ATT_HD:(g + 1) * ATT_HD] = out[g * tq:(g + 1) * tq, :].astype(bf16)


def _flash(q, k, v, lay):
    R = q.shape[0]
    B, T, Tc = lay.B, lay.T, lay.Tc
    tq = tkv = lay.tm
    ncq, nlq = Tc // tq, T // tq
    nck, nlk = Tc // tkv, T // tkv
    gw = ATT_G * ATT_HD

    def q_map(b, kv, qi, j):
        return (jnp.where(qi < ncq, b * ncq + qi, B * ncq + b * nlq + (qi - ncq)), kv)

    def k_map(b, kv, qi, j):
        je = jnp.where(qi < ncq, jnp.minimum(j, nck - 1), j)
        return (jnp.where(je < nck, b * nck + je, B * nck + b * nlk + (je - nck)), kv)

    body = functools.partial(_flash_body, tq=tq, ncq=ncq, nck=nck)
    return pl.pallas_call(
        body,
        out_shape=jax.ShapeDtypeStruct((R, ATT_H * ATT_HD), bf16),
        grid=(B, ATT_KVH, ncq + nlq, nck + nlk),
        in_specs=[pl.BlockSpec((tq, gw), q_map),
                  pl.BlockSpec((tkv, ATT_HD), k_map),
                  pl.BlockSpec((tkv, ATT_HD), k_map)],
        out_specs=pl.BlockSpec((tq, gw), q_map),
        scratch_shapes=[pltpu.VMEM((ATT_G * tq, 1), f32), pltpu.VMEM((ATT_G * tq, 1), f32),
                        pltpu.VMEM((ATT_G * tq, ATT_HD), f32)],
        compiler_params=_cparams("parallel", "parallel", "parallel", "arbitrary"),
        name="flash_gqa",
    )(q, k, v)


def _outproj_body(a_ref, x_ref, m_ref, w_ref, o_ref):
    o_ref[...] = x_ref[...] + m_ref[0][2:3, :] * _dot(a_ref[...], w_ref[...])


def _outproj(a, xs, mods_l, w, lay):
    R, D = xs.shape
    tm = lay.tm
    K = a.shape[1]
    return pl.pallas_call(
        _outproj_body,
        out_shape=jax.ShapeDtypeStruct((R, D), f32),
        grid=(R // tm,),
        in_specs=[pl.BlockSpec((tm, K), lambda i: (i, 0)),
                  pl.BlockSpec((tm, D), lambda i: (i, 0)),
                  pl.BlockSpec((1, 6, D), lambda i: (lay.mod_row(i * tm), 0, 0)),
                  _const_spec(w.shape)],
        out_specs=pl.BlockSpec((tm, D), lambda i: (i, 0)),
        compiler_params=_cparams("parallel"),
        name="att_outproj",
    )(a, xs, mods_l, w)


REC_MAIN = 2 * DN_H * DN_DK + 2 * DN_H * DN_DV + 2 * GLA_H * GLA_DK + 2 * GLA_H * GLA_DV
REC_QKV = 2 * DN_H * DN_DK + DN_H * DN_DV
OFF_DZ = REC_QKV
OFF_GQ = OFF_DZ + DN_H * DN_DV
OFF_GK = OFF_GQ + GLA_H * GLA_DK
OFF_GV = OFF_GK + GLA_H * GLA_DK
OFF_GR = OFF_GV + GLA_H * GLA_DV
SM_DA = 0
SM_DB = 2 * DN_H
SM_GG = 4 * DN_H


def _recin_body(x_ref, m_ref, wm_ref, ws_ref, ym_ref, ys_ref):
    h = _normmod(x_ref[...], m_ref[0], 0, 1).astype(bf16)
    ym_ref[...] = _dot(h, wm_ref[...])
    ys_ref[...] = _dot(h, ws_ref[...])


def _recin(xs, mods_l, wm, ws, lay):
    R, D = xs.shape
    tm = lay.tm
    return pl.pallas_call(
        _recin_body,
        out_shape=(jax.ShapeDtypeStruct((R, REC_MAIN), f32), jax.ShapeDtypeStruct((R, LANES), f32)),
        grid=(R // tm,),
        in_specs=[pl.BlockSpec((tm, D), lambda i: (i, 0)),
                  pl.BlockSpec((1, 6, D), lambda i: (lay.mod_row(i * tm), 0, 0)),
                  _const_spec(wm.shape), _const_spec(ws.shape)],
        out_specs=(pl.BlockSpec((tm, REC_MAIN), lambda i: (i, 0)), pl.BlockSpec((tm, LANES), lambda i: (i, 0))),
        compiler_params=_cparams("parallel"),
        name="rec_inproj",
    )(xs, mods_l, wm, ws)


def _recprep_body(yp_ref, y_ref, yn_ref, ys_ref, cw_ref, av_ref, dtb_ref, w2h_ref, w2l_ref, b2_ref,
                  q_ref, k_ref, v_ref, g_ref, la_ref, e_sc, *, lay, halo):
    tm = lay.tm
    row0 = pl.program_id(0) * tm
    e_sc[0:halo, :] = yp_ref[...]
    e_sc[halo:halo + tm, :] = y_ref[...]
    e_sc[halo + tm:halo + tm + halo, :] = yn_ref[...]
    rowi = lax.broadcasted_iota(jnp.int32, (tm, 1), 0)
    kill_prev = jnp.logical_and(rowi == 0, lay.is_seq_start(row0))
    kill_next = jnp.logical_and(rowi == tm - 1, jnp.logical_or(lay.is_seq_start(row0 + tm), row0 + tm == lay.R))
    cw = cw_ref[...]
    gm1 = jnp.where(kill_prev, 0.0, e_sc[pl.ds(halo - 1, tm), :])
    g0 = e_sc[pl.ds(halo, tm), :]
    gp1 = jnp.where(kill_next, 0.0, e_sc[pl.ds(halo + 1, tm), :])
    s = _silu(gm1 * cw[0:1, :] + g0 * cw[1:2, :] + gp1 * cw[2:3, :])
    nqk = DN_H * DN_DK
    for h in range(DN_H):
        z = s[:, h * DN_DK:(h + 1) * DN_DK]
        q_ref[:, h * DN_DK:(h + 1) * DN_DK] = z * lax.rsqrt(jnp.sum(z * z, axis=-1, keepdims=True) + EPS) * (DN_DK ** -0.5)
        z = s[:, nqk + h * DN_DK:nqk + (h + 1) * DN_DK]
        k_ref[:, h * DN_DK:(h + 1) * DN_DK] = z * lax.rsqrt(jnp.sum(z * z, axis=-1, keepdims=True) + EPS)
    v_ref[...] = s[:, 2 * nqk:]
    z = ys_ref[...]
    lane = lax.broadcasted_iota(jnp.int32, z.shape, 1)
    g = -jnp.exp(av_ref[...]) * jax.nn.softplus(z + dtb_ref[...])
    beta = jax.nn.sigmoid(z)
    g_ref[...] = jnp.where(lane < SM_DB, g, jnp.where(lane < SM_GG, beta, 0.0))
    zh, zl = _split2(z)
    for d in range(2):
        la = _dot(zh, w2h_ref[d]) + _dot(zl, w2h_ref[d]) + _dot(zh, w2l_ref[d]) + b2_ref[d]
        la_ref[d] = jax.nn.log_sigmoid(la) / GLA_TAU


def _recprep(ym, ys, cw, av, dtb, w2h, w2l, b2, lay):
    R = ym.shape[0]
    tm = lay.tm
    halo = 8
    nh = tm // halo
    nhb = R // halo
    nd = DN_H * DN_DK
    body = functools.partial(_recprep_body, lay=lay, halo=halo)
    return pl.pallas_call(
        body,
        out_shape=(jax.ShapeDtypeStruct((R, nd), f32), jax.ShapeDtypeStruct((R, nd), f32),
                   jax.ShapeDtypeStruct((R, DN_H * DN_DV), f32), jax.ShapeDtypeStruct((R, LANES), f32),
                   jax.ShapeDtypeStruct((2, R, GLA_H * GLA_DK), f32)),
        grid=(R // tm,),
        in_specs=[pl.BlockSpec((halo, REC_QKV), lambda i: (jnp.maximum(i * nh - 1, 0), 0)),
                  pl.BlockSpec((tm, REC_QKV), lambda i: (i, 0)),
                  pl.BlockSpec((halo, REC_QKV), lambda i: (jnp.minimum((i + 1) * nh, nhb - 1), 0)),
                  pl.BlockSpec((tm, LANES), lambda i: (i, 0)),
                  _const_spec(cw.shape), _const_spec(av.shape), _const_spec(dtb.shape),
                  _const_spec(w2h.shape), _const_spec(w2l.shape), _const_spec(b2.shape)],
        out_specs=(pl.BlockSpec((tm, nd), lambda i: (i, 0)), pl.BlockSpec((tm, nd), lambda i: (i, 0)),
                   pl.BlockSpec((tm, DN_H * DN_DV), lambda i: (i, 0)), pl.BlockSpec((tm, LANES), lambda i: (i, 0)),
                   pl.BlockSpec((2, tm, GLA_H * GLA_DK), lambda i: (0, i, 0))),
        scratch_shapes=[pltpu.VMEM((tm + 2 * halo, REC_QKV), f32)],
        compiler_params=_cparams("parallel"),
        name="rec_prep",
    )(ym, ym, ym, ys, cw, av, dtb, w2h, w2l, b2)


def _tri_masks(rev, width):
    ri = lax.broadcasted_iota(jnp.int32, (CHUNK, width), 0)
    ci = lax.broadcasted_iota(jnp.int32, (CHUNK, width), 1) % CHUNK
    if rev:
        return ri <= ci, ri < ci, ri >= ci
    return ri >= ci, ri > ci, ri <= ci


def _block_diag(x, nblk):
    w = x.shape[1]
    t = jnp.concatenate([x] * nblk, axis=0)
    rb = lax.broadcasted_iota(jnp.int32, t.shape, 0) // CHUNK
    cb = lax.broadcasted_iota(jnp.int32, t.shape, 1) // (w // nblk)
    return jnp.where(rb == cb, t, jnp.zeros_like(t))


def _delta_dir(rev, q_ref, k_ref, v_ref, g_ref, o_ref, s_sc, d):
    H = DN_H
    gt = g_ref[...]
    q_all = q_ref[...]
    k_all = k_ref[...]
    v_all = v_ref[...]
    incl, strict, incl_t = _tri_masks(rev, CHUNK)
    incl4, strict4, incl_t4 = _tri_masks(rev, H * CHUNK)
    mb = incl.astype(bf16)
    gcol = [gt[:, SM_DA + d * H + h:SM_DA + d * H + h + 1] for h in range(H)]
    bcol = [gt[:, SM_DB + d * H + h:SM_DB + d * H + h + 1] for h in range(H)]
    g128 = jnp.concatenate([jnp.broadcast_to(c, (CHUNK, DN_DK)) for c in gcol], axis=1)
    b128 = jnp.concatenate([jnp.broadcast_to(c, (CHUNK, DN_DK)) for c in bcol], axis=1)
    g64 = jnp.concatenate([jnp.broadcast_to(c, (CHUNK, CHUNK)) for c in gcol], axis=1)
    b64 = jnp.concatenate([jnp.broadcast_to(c, (CHUNK, CHUNK)) for c in bcol], axis=1)
    hi, lo = _split2(g128)
    G = _dot(jnp.concatenate([mb, mb], axis=1), jnp.concatenate([hi, lo], axis=0))
    hi6, lo6 = _split2(g64)
    zero = jnp.zeros_like(hi6)
    rhs = jnp.concatenate([hi6, lo6, jnp.where(incl_t4, hi6, zero), jnp.where(incl_t4, lo6, zero)], axis=0)
    neg1 = jnp.full((CHUNK, CHUNK), -1.0, bf16)
    diff = _dot(jnp.concatenate([mb, mb, neg1, neg1], axis=1), rhs)
    decay = jnp.where(incl4, jnp.exp(diff), 0.0)
    k_bf = k_all.astype(bf16)
    kq = jnp.concatenate([k_bf, q_all.astype(bf16)], axis=0)
    kkqk = _dot_nt(kq, _block_diag(k_bf, H))
    kk = kkqk[:CHUNK]
    qk = kkqk[CHUNK:]
    a = jnp.where(strict4, b64 * kk * decay, 0.0)
    ri = lax.broadcasted_iota(jnp.int32, (CHUNK, H * CHUNK), 0)
    ci = lax.broadcasted_iota(jnp.int32, (CHUNK, H * CHUNK), 1) % CHUNK
    p = jnp.where(ri == ci, 1.0, 0.0) - jnp.where(ri // 2 == ci // 2, a, 0.0)
    m = 2
    while m < CHUNK:
        a_o = jnp.where(jnp.logical_and(ri // (2 * m) == ci // (2 * m), ri // m != ci // m), a, 0.0)
        x = _dot(p.astype(bf16), _block_diag(a_o.astype(bf16), H))
        p = p - _dot(x.astype(bf16), _block_diag(p.astype(bf16), H))
        m *= 2
    tb = p.astype(bf16)
    intra = (qk * decay).astype(bf16)
    last = 0 if rev else CHUNK - 1
    g_last = G[last:last + 1, :]
    e_g = jnp.exp(G)
    e_last = jnp.exp(g_last)
    vbeta = v_all * b128
    kbg = k_all * b128 * e_g
    qd = q_all * e_g
    kd = k_all * jnp.exp(g_last - G)
    for h in range(H):
        sl = slice(h * DN_DK, (h + 1) * DN_DK)
        sc = slice(h * CHUNK, (h + 1) * CHUNK)
        uw = _dot(tb[:, sc], jnp.concatenate([vbeta[:, sl], kbg[:, sl]], axis=1).astype(bf16))
        u = uw[:, :DN_DV]
        w = uw[:, DN_DV:]
        S = s_sc[d, h]
        r1 = _dot(jnp.concatenate([w, qd[:, sl]], axis=0).astype(bf16), S.astype(bf16))
        vnb = (u - r1[:CHUNK]).astype(bf16)
        o_ref[:, sl] = r1[CHUNK:] + _dot(intra[:, sc], vnb)
        s_sc[d, h] = S * e_last[:, sl] + _dot_tn(kd[:, sl].astype(bf16), vnb)


def _delta_body(qf, kf, vf, gf, qb, kb, vb, gb, of_ref, ob_ref, s_sc):
    @pl.when(pl.program_id(1) == 0)
    def _():
        s_sc[...] = jnp.zeros_like(s_sc)

    _delta_dir(False, qf, kf, vf, gf, of_ref, s_sc, 0)
    _delta_dir(True, qb, kb, vb, gb, ob_ref, s_sc, 1)


def _chunk_maps(lay):
    B = lay.B
    ncc, ncl = lay.Tc // CHUNK, lay.T // CHUNK

    def fwd(b, n):
        return jnp.where(n < ncc, b * ncc + n, B * ncc + b * ncl + (n - ncc))

    def bwd(b, n):
        return jnp.where(n < ncc, b * ncc + (ncc - 1 - n), B * ncc + b * ncl + (ncl - 1 - (n - ncc)))

    return fwd, bwd, ncc + ncl


def _delta(q, k, v, gates, lay):
    R = q.shape[0]
    fwd, bwd, nstep = _chunk_maps(lay)
    wq = DN_H * DN_DK
    wv = DN_H * DN_DV

    def spec(w, m):
        return pl.BlockSpec((CHUNK, w), lambda b, n: (m(b, n), 0))

    return pl.pallas_call(
        _delta_body,
        out_shape=(jax.ShapeDtypeStruct((R, wv), f32), jax.ShapeDtypeStruct((R, wv), f32)),
        grid=(lay.B, nstep),
        in_specs=[spec(wq, fwd), spec(wq, fwd), spec(wv, fwd), spec(LANES, fwd),
                  spec(wq, bwd), spec(wq, bwd), spec(wv, bwd), spec(LANES, bwd)],
        out_specs=(spec(wv, fwd), spec(wv, bwd)),
        scratch_shapes=[pltpu.VMEM((2, DN_H, DN_DK, DN_DV), f32)],
        compiler_params=_cparams("parallel", "arbitrary"),
        name="delta_scan",
    )(q, k, v, gates, q, k, v, gates)


def _gla_dir(rev, q_ref, k_ref, v_ref, la_ref, o_ref, st_sc, d):
    H = GLA_H
    la = la_ref[0]
    incl4 = _tri_masks(rev, H * CHUNK)[0]
    mb = _tri_masks(rev, CHUNK)[0].astype(bf16)
    hi, lo = _split2(la)
    b = _dot(jnp.concatenate([mb, mb], axis=1), jnp.concatenate([hi, lo], axis=0))
    mid = CHUNK - 1 - CHUNK // 2 if rev else CHUNK // 2
    last = 0 if rev else CHUNK - 1
    b_mid = b[mid:mid + 1, :]
    b_last = b[last:last + 1, :]
    q = q_ref[...] * (GLA_DK ** -0.5)
    k = k_ref[...]
    v_bf = v_ref[...].astype(bf16)
    qe = (q * jnp.exp(b - b_mid)).astype(bf16)
    ke = (k * jnp.exp(b_mid - b)).astype(bf16)
    att = jnp.where(incl4, _dot_nt(qe, _block_diag(ke, H)), 0.0).astype(bf16)
    o_intra = _dot(att, _block_diag(v_bf, H))
    qi = (q * jnp.exp(b)).astype(bf16)
    ks = (k * jnp.exp(b_last - b)).astype(bf16)
    st = st_sc[d]
    st_bf = st.astype(bf16)
    kv = []
    for h in range(H):
        sk = slice(h * GLA_DK, (h + 1) * GLA_DK)
        sv = slice(h * GLA_DV, (h + 1) * GLA_DV)
        o_ref[:, sv] = o_intra[:, sv] + _dot_nt(qi[:, sk], st_bf[:, sk])
        kv.append(_dot_tn(v_bf[:, sv], ks[:, sk]))
    st_sc[d] = st * jnp.exp(b_last) + jnp.concatenate(kv, axis=1)


def _gla_body(qf, kf, vf, lf, qb, kb, vb, lb, of_ref, ob_ref, st_sc):
    @pl.when(pl.program_id(1) == 0)
    def _():
        st_sc[...] = jnp.zeros_like(st_sc)

    _gla_dir(False, qf, kf, vf, lf, of_ref, st_sc, 0)
    _gla_dir(True, qb, kb, vb, lb, ob_ref, st_sc, 1)


def _gla(ym, la, lay):
    R = ym.shape[0]
    fwd, bwd, nstep = _chunk_maps(lay)
    wk = GLA_H * GLA_DK
    wv = GLA_H * GLA_DV

    def spec(w, m, col):
        return pl.BlockSpec((CHUNK, w), lambda b, n: (m(b, n), col))

    def la_spec(m, d):
        return pl.BlockSpec((1, CHUNK, wk), lambda b, n: (d, m(b, n), 0))

    cq, ck, cv = OFF_GQ // wk, OFF_GK // wk, OFF_GV // wv
    return pl.pallas_call(
        _gla_body,
        out_shape=(jax.ShapeDtypeStruct((R, wv), f32), jax.ShapeDtypeStruct((R, wv), f32)),
        grid=(lay.B, nstep),
        in_specs=[spec(wk, fwd, cq), spec(wk, fwd, ck), spec(wv, fwd, cv), la_spec(fwd, 0),
                  spec(wk, bwd, cq), spec(wk, bwd, ck), spec(wv, bwd, cv), la_spec(bwd, 1)],
        out_specs=(spec(wv, fwd, 0), spec(wv, bwd, 0)),
        scratch_shapes=[pltpu.VMEM((2, GLA_DV, GLA_H * GLA_DK), f32)],
        compiler_params=_cparams("parallel", "arbitrary"),
        name="gla_scan",
    )(ym, ym, ym, la, ym, ym, ym, la)


def _recout_body(df_ref, db_ref, gf_ref, gb_ref, z_ref, r_ref, x_ref, m_ref, dn_ref, gn_ref, w_ref, o_ref):
    dn = df_ref[...] + db_ref[...]
    gl = gf_ref[...] + gb_ref[...]
    z = z_ref[...]
    r = r_ref[...]
    parts = []
    for src, gate, gain, hd, nh in ((dn, z, dn_ref[...], DN_DV, DN_H), (gl, r, gn_ref[...], GLA_DV, GLA_H)):
        for h in range(nh):
            a = src[:, h * hd:(h + 1) * hd]
            a = a * lax.rsqrt(jnp.mean(a * a, axis=-1, keepdims=True) + EPS) * gain
            parts.append(a * _silu(gate[:, h * hd:(h + 1) * hd]))
    mix = jnp.concatenate(parts, axis=1).astype(bf16)
    o_ref[...] = x_ref[...] + m_ref[0][2:3, :] * _dot(mix, w_ref[...])


def _recout(dn_f, dn_b, gl_f, gl_b, ym, xs, mods_l, dn_norm, gla_norm, w, lay):
    R, D = xs.shape
    tm = lay.tm
    wd = DN_H * DN_DV
    wg = GLA_H * GLA_DV
    row = lambda i: (i, 0)
    return pl.pallas_call(
        _recout_body,
        out_shape=jax.ShapeDtypeStruct((R, D), f32),
        grid=(R // tm,),
        in_specs=[pl.BlockSpec((tm, wd), row), pl.BlockSpec((tm, wd), row),
                  pl.BlockSpec((tm, wg), row), pl.BlockSpec((tm, wg), row),
                  pl.BlockSpec((tm, wd), lambda i: (i, OFF_DZ // wd)),
                  pl.BlockSpec((tm, wg), lambda i: (i, OFF_GR // wg)),
                  pl.BlockSpec((tm, D), row),
                  pl.BlockSpec((1, 6, D), lambda i: (lay.mod_row(i * tm), 0, 0)),
                  _const_spec(dn_norm.shape), _const_spec(gla_norm.shape), _const_spec(w.shape)],
        out_specs=pl.BlockSpec((tm, D), row),
        compiler_params=_cparams("parallel"),
        name="rec_outproj",
    )(dn_f, dn_b, gl_f, gl_b, ym, ym, xs, mods_l, dn_norm, gla_norm, w)


def _final_body(x_ref, g_ref, o_ref):
    x = x_ref[...]
    o_ref[...] = x * lax.rsqrt(jnp.mean(x * x, axis=-1, keepdims=True) + EPS) * g_ref[...]


def _final(xs, gain, lay):
    R, D = xs.shape
    tm = lay.tm
    off = lay.B * lay.Tc // tm
    n = lay.B * lay.T
    return pl.pallas_call(
        _final_body,
        out_shape=jax.ShapeDtypeStruct((n, D), f32),
        grid=(n // tm,),
        in_specs=[pl.BlockSpec((tm, D), lambda i: (i + off, 0)), _const_spec(gain.shape)],
        out_specs=pl.BlockSpec((tm, D), lambda i: (i, 0)),
        compiler_params=_cparams("parallel"),
        name="final_norm",
    )(xs, gain)


def _rec_weights(w_in):
    sizes = (DN_H * DN_DK, DN_H * DN_DK, DN_H * DN_DV, DN_H * DN_DV, 2 * DN_H, 2 * DN_H,
             GLA_H * GLA_DK, GLA_H * GLA_DK, GLA_H * GLA_DV, GLA_H * GLA_DV, 2 * GLA_RANK)
    offs = [0]
    for s in sizes:
        offs.append(offs[-1] + s)
    seg = [w_in[:, offs[i]:offs[i + 1]] for i in range(len(sizes))]
    dq, dk, dv, dz, da, db, gq, gk, gv, gr, gg = seg
    main = jnp.concatenate([dq, dk, dv, dz, gq, gk, gv, gr], axis=1).astype(bf16)
    small = jnp.concatenate([da, db, gg], axis=1)
    small = jnp.pad(small, ((0, 0), (0, LANES - small.shape[1]))).astype(bf16)
    return main, small


def _pad_lanes(v, off=0):
    v = v.reshape(1, -1)
    return jnp.pad(v, ((0, 0), (off, LANES - off - v.shape[1])))


def _gla_gate_weights(w2):
    out = jnp.zeros((2, LANES, w2.shape[2]), f32)
    for d in range(2):
        out = out.at[d, SM_GG + d * GLA_RANK:SM_GG + (d + 1) * GLA_RANK, :].set(w2[d])
    hi = out.astype(bf16)
    lo = (out - hi.astype(f32)).astype(bf16)
    return hi, lo


def _ffn_weights(w_up, conv, w_down):
    D, F2 = w_up.shape
    F = F2 // 2
    n = F // FFN_CHUNK
    wg = w_up[:, :F].astype(bf16).reshape(D, n, FFN_CHUNK).transpose(1, 0, 2)
    wv = w_up[:, F:].astype(bf16).reshape(D, n, FFN_CHUNK).transpose(1, 0, 2)
    cw = conv.reshape(3, n, FFN_CHUNK).transpose(1, 0, 2)
    wd = w_down.astype(bf16).reshape(n, FFN_CHUNK, D)
    return wg, wv, cw, wd


def _rope_tables(lay):
    T, Tc, B = lay.T, lay.Tc, lay.B
    t = jnp.arange(T, dtype=jnp.int32)
    row = (t // GRID_W).astype(f32)
    col = (t % GRID_W).astype(f32)
    inv_freq = ROPE_THETA ** (-jnp.arange(ROPE_PAIRS, dtype=f32) / ROPE_PAIRS)
    ar = row[:, None] * inv_freq
    ac = col[:, None] * inv_freq
    cos = jnp.concatenate([jnp.cos(ar), jnp.cos(ar), jnp.cos(ac), jnp.cos(ac)], axis=1)
    sin = jnp.concatenate([-jnp.sin(ar), jnp.sin(ar), -jnp.sin(ac), jnp.sin(ac)], axis=1)
    cos = jnp.concatenate([jnp.ones((B * Tc, ATT_HD), f32)] + [cos] * B, axis=0)
    sin = jnp.concatenate([jnp.zeros((B * Tc, ATT_HD), f32)] + [sin] * B, axis=0)
    return cos, sin


def kernel(x, c, ctx, c_ctx, mod_w, mod_b, rec_w_in, rec_conv, dn_a_log, dn_dt_bias, dn_norm, gla_w2, gla_b2,
           gla_norm, rec_w_out, att_w_qkv, att_q_norm, att_k_norm, att_w_out, ffn_w_up, ffn_conv, ffn_w_down,
           final_norm):
    B, T, D = x.shape
    Tc = ctx.shape[1]
    depth = mod_w.shape[0]
    lay = _Layout(B, T, Tc)
    assert B + 1 <= MOD_ROWS
    xs = jnp.concatenate([ctx.reshape(B * Tc, D), x.reshape(B * T, D)], axis=0)
    cond = jnp.concatenate([c, c_ctx[None], jnp.zeros((MOD_ROWS - B - 1, D), f32)], axis=0)
    mods = _mods(cond, mod_w, mod_b)
    cos, sin = _rope_tables(lay)

    for i in range(depth):
        ml = mods[i]
        if i % 2 == 0:
            e = i // 2
            wm, ws = _rec_weights(rec_w_in[e])
            ym, ys = _recin(xs, ml, wm, ws, lay)
            w2h, w2l = _gla_gate_weights(gla_w2[e])
            q, k, v, gates, la = _recprep(
                ym, ys, rec_conv[e], _pad_lanes(dn_a_log[e], SM_DA), _pad_lanes(dn_dt_bias[e], SM_DA),
                w2h, w2l, gla_b2[e].reshape(2, 1, -1), lay)
            dn_f, dn_b = _delta(q, k, v, gates, lay)
            gl_f, gl_b = _gla(ym, la, lay)
            xs = _recout(dn_f, dn_b, gl_f, gl_b, ym, xs, ml, dn_norm[e].reshape(1, -1), gla_norm[e].reshape(1, -1),
                         rec_w_out[e].astype(bf16), lay)
        else:
            o = i // 2
            q, k, v = _qkv(xs, ml, att_w_qkv[o].astype(bf16), att_q_norm[o].reshape(1, -1),
                           att_k_norm[o].reshape(1, -1), cos, sin, lay)
            a = _flash(q, k, v, lay)
            xs = _outproj(a, xs, ml, att_w_out[o].astype(bf16), lay)
        wg, wv, cw, wd = _ffn_weights(ffn_w_up[i], ffn_conv[i], ffn_w_down[i])
        xs = _ffn(xs, ml, wg, wv, cw, wd, lay)

    out = _final(xs, final_norm.reshape(1, -1), lay)
    return out.reshape(B, T, D)
```

```python
import functools
import math

import jax
import jax.numpy as jnp
from jax import lax
from jax.experimental import pallas as pl
from jax.experimental.pallas import tpu as pltpu

f32 = jnp.float32
bf16 = jnp.bfloat16

EPS = 1e-6
GRID_W = 64
DN_H = 4
DN_DK = 128
DN_DV = 128
GLA_H = 4
GLA_DK = 64
GLA_DV = 128
GLA_RANK = 16
GLA_TAU = 16.0
CHUNK = 64
ATT_H = 8
ATT_KVH = 2
ATT_G = ATT_H // ATT_KVH
ATT_HD = 128
ROPE_THETA = 10000.0
ROPE_PAIRS = ATT_HD // 4
LOG2E = math.log2(math.e)
FFN_CHUNK = 256
MOD_ROWS = 8
LANES = 128
VMEM_LIMIT = 56 * 1024 * 1024


def _cparams(*sem):
    return pltpu.CompilerParams(dimension_semantics=sem, vmem_limit_bytes=VMEM_LIMIT)


def _const_spec(shape):
    nd = len(shape)
    return pl.BlockSpec(shape, lambda *_: (0,) * nd, pipeline_mode=pl.Buffered(1))


class _Layout:
    def __init__(self, B, T, Tc):
        self.B, self.T, self.Tc = B, T, Tc
        self.R = B * Tc + B * T
        self.tm = min(256, Tc)
        assert Tc % self.tm == 0 and T % self.tm == 0 and T % CHUNK == 0 and Tc % CHUNK == 0
        self.tmf = min(512, B * Tc)
        assert (B * Tc) % self.tmf == 0 and T % self.tmf == 0
        self.seq_starts = [b * Tc for b in range(B)] + [B * Tc + b * T for b in range(B)]

    def mod_row(self, row0):
        return jnp.where(row0 < self.B * self.Tc, self.B, (row0 - self.B * self.Tc) // self.T)

    def is_seq_start(self, row0):
        return jnp.where(row0 < self.B * self.Tc, row0 % self.Tc == 0, (row0 - self.B * self.Tc) % self.T == 0)

    def edge_masks(self, rows):
        first = rows == self.seq_starts[0]
        last = rows == self.R - 1
        for s in self.seq_starts[1:]:
            first = jnp.logical_or(first, rows == s)
            last = jnp.logical_or(last, rows == s - 1)
        return first, last


def _normmod(x, m, shift_i, scale_i):
    ms = jnp.mean(x * x, axis=-1, keepdims=True)
    return (x * lax.rsqrt(ms + EPS)) * (1.0 + m[scale_i:scale_i + 1, :]) + m[shift_i:shift_i + 1, :]


def _silu(x):
    return x * jax.nn.sigmoid(x)


def _split2(x):
    hi = x.astype(bf16)
    lo = (x - hi.astype(f32)).astype(bf16)
    return hi, lo


def _dot(a, b):
    return jnp.dot(a, b, preferred_element_type=f32)


def _dot_nt(a, b):
    return lax.dot_general(a, b, (((1,), (1,)), ((), ())), preferred_element_type=f32)


def _dot_tn(a, b):
    return lax.dot_general(a, b, (((0,), (0,)), ((), ())), preferred_element_type=f32)


def _mod_body(c_ref, w_ref, b_ref, o_ref):
    a = _silu(c_ref[...]).astype(bf16)
    o_ref[0] = _dot(a, w_ref[0].astype(bf16)) + b_ref[0]


def _mods(cond, mod_w, mod_b):
    depth, D, D6 = mod_w.shape
    tn = 1536
    out = pl.pallas_call(
        _mod_body,
        out_shape=jax.ShapeDtypeStruct((depth, MOD_ROWS, D6), f32),
        grid=(depth, D6 // tn),
        in_specs=[pl.BlockSpec((MOD_ROWS, D), lambda l, j: (0, 0)),
                  pl.BlockSpec((1, D, tn), lambda l, j: (l, 0, j)),
                  pl.BlockSpec((1, 1, tn), lambda l, j: (l, 0, j))],
        out_specs=pl.BlockSpec((1, MOD_ROWS, tn), lambda l, j: (l, 0, j)),
        compiler_params=_cparams("parallel", "parallel"),
        name="mods",
    )(cond, mod_w, mod_b.reshape(depth, 1, D6))
    return out.reshape(depth, MOD_ROWS, 6, D)


def _ffn_body(xp_ref, x_ref, xn_ref, m_ref, wg_ref, wv_ref, cw_ref, wd_ref, o_ref, h_sc, g_sc, acc_sc, *, lay, halo, ncf):
    tm = lay.tmf
    row0 = pl.program_id(0) * tm
    m = m_ref[0]
    x = x_ref[...]
    xe = jnp.concatenate([xp_ref[...], x, xn_ref[...]], axis=0)
    h_sc[...] = _normmod(xe, m, 3, 4).astype(bf16)
    kill_prev, kill_next = lay.edge_masks(row0 + lax.broadcasted_iota(jnp.int32, (tm, 1), 0))
    acc_sc[...] = jnp.zeros_like(acc_sc)

    def step(c, carry):
        g_sc[...] = _dot(h_sc[...], wg_ref[c])
        val = _dot(h_sc[pl.ds(halo, tm), :], wv_ref[c])
        cw = cw_ref[c]
        gm1 = jnp.where(kill_prev, 0.0, g_sc[pl.ds(halo - 1, tm), :])
        g0 = g_sc[pl.ds(halo, tm), :]
        gp1 = jnp.where(kill_next, 0.0, g_sc[pl.ds(halo + 1, tm), :])
        gate = gm1 * cw[0:1, :] + g0 * cw[1:2, :] + gp1 * cw[2:3, :]
        act = (_silu(gate) * val).astype(bf16)
        acc_sc[...] += _dot(act, wd_ref[c])
        return carry

    lax.fori_loop(0, ncf, step, 0)
    o_ref[...] = x + m[5:6, :] * acc_sc[...]


def _ffn(xs, mods_l, wg, wv, cw, wd, lay):
    R, D = xs.shape
    tm = lay.tmf
    halo = 16
    ncf = wg.shape[0]
    cf = wg.shape[2]
    nh = tm // halo
    nhb = R // halo
    body = functools.partial(_ffn_body, lay=lay, halo=halo, ncf=ncf)
    return pl.pallas_call(
        body,
        out_shape=jax.ShapeDtypeStruct((R, D), f32),
        grid=(R // tm,),
        in_specs=[pl.BlockSpec((halo, D), lambda i: (jnp.maximum(i * nh - 1, 0), 0)),
                  pl.BlockSpec((tm, D), lambda i: (i, 0)),
                  pl.BlockSpec((halo, D), lambda i: (jnp.minimum((i + 1) * nh, nhb - 1), 0)),
                  pl.BlockSpec((1, 6, D), lambda i: (lay.mod_row(i * tm), 0, 0)),
                  _const_spec(wg.shape), _const_spec(wv.shape), _const_spec(cw.shape), _const_spec(wd.shape)],
        out_specs=pl.BlockSpec((tm, D), lambda i: (i, 0)),
        scratch_shapes=[pltpu.VMEM((tm + 2 * halo, D), bf16),
                        pltpu.VMEM((tm + 2 * halo, cf), f32),
                        pltpu.VMEM((tm, D), f32)],
        compiler_params=_cparams("parallel"),
        name="conv_ffn",
    )(xs, xs, xs, mods_l, wg, wv, cw, wd)


def _qkv_body(x_ref, m_ref, w_ref, qn_ref, kn_ref, cos_ref, sin_ref, q_ref, k_ref, v_ref):
    h = _normmod(x_ref[...], m_ref[0], 0, 1).astype(bf16)
    y = _dot(h, w_ref[...])
    cos = cos_ref[...]
    sin = sin_ref[...]
    lane = lax.broadcasted_iota(jnp.int32, cos.shape, 1)
    first = (lane % (2 * ROPE_PAIRS)) < ROPE_PAIRS

    def norm_rope(z, gain):
        z = z * lax.rsqrt(jnp.mean(z * z, axis=-1, keepdims=True) + EPS) * gain
        partner = jnp.where(first, pltpu.roll(z, LANES - ROPE_PAIRS, axis=1), pltpu.roll(z, ROPE_PAIRS, axis=1))
        return z * cos + partner * sin

    nq = ATT_H * ATT_HD
    for hh in range(ATT_H):
        z = norm_rope(y[:, hh * ATT_HD:(hh + 1) * ATT_HD], qn_ref[...])
        q_ref[:, hh * ATT_HD:(hh + 1) * ATT_HD] = (z * (ATT_HD ** -0.5 * LOG2E)).astype(bf16)
    for kk in range(ATT_KVH):
        z = norm_rope(y[:, nq + kk * ATT_HD:nq + (kk + 1) * ATT_HD], kn_ref[...])
        k_ref[:, kk * ATT_HD:(kk + 1) * ATT_HD] = z.astype(bf16)
    nk = nq + ATT_KVH * ATT_HD
    ones = jnp.ones((y.shape[0], ATT_HD), bf16)
    for kk in range(ATT_KVH):
        v_ref[:, 2 * kk * ATT_HD:(2 * kk + 1) * ATT_HD] = y[:, nk + kk * ATT_HD:nk + (kk + 1) * ATT_HD].astype(bf16)
        v_ref[:, (2 * kk + 1) * ATT_HD:(2 * kk + 2) * ATT_HD] = ones


def _qkv(xs, mods_l, w, qn, kn, cos, sin, lay):
    R, D = xs.shape
    tm = lay.tmf
    nq = ATT_H * ATT_HD
    nkv = ATT_KVH * ATT_HD
    return pl.pallas_call(
        _qkv_body,
        out_shape=(jax.ShapeDtypeStruct((R, nq), bf16), jax.ShapeDtypeStruct((R, nkv), bf16),
                   jax.ShapeDtypeStruct((R, 2 * nkv), bf16)),
        grid=(R // tm,),
        in_specs=[pl.BlockSpec((tm, D), lambda i: (i, 0)),
                  pl.BlockSpec((1, 6, D), lambda i: (lay.mod_row(i * tm), 0, 0)),
                  _const_spec(w.shape), _const_spec(qn.shape), _const_spec(kn.shape),
                  pl.BlockSpec((tm, ATT_HD), lambda i: (i, 0)),
                  pl.BlockSpec((tm, ATT_HD), lambda i: (i, 0))],
        out_specs=(pl.BlockSpec((tm, nq), lambda i: (i, 0)), pl.BlockSpec((tm, nkv), lambda i: (i, 0)),
                   pl.BlockSpec((tm, 2 * nkv), lambda i: (i, 0))),
        compiler_params=_cparams("parallel"),
        name="qkv_rope",
    )(xs, mods_l, w, qn, kn, cos, sin)


def _flash_body(q_ref, kc_ref, vc_ref, kl_ref, vl_ref, o_ref, m_sc, acc_sc, *, tq, ncq, ck):
    qi = pl.program_id(2)
    q = q_ref[...]
    q4 = jnp.concatenate([q[:, g * ATT_HD:(g + 1) * ATT_HD] for g in range(ATT_G)], axis=0)
    m_sc[...] = jnp.full_like(m_sc, -jnp.inf)
    acc_sc[...] = jnp.zeros_like(acc_sc)

    def chunk(k, v):
        s = _dot_nt(q4, k)
        m_prev = m_sc[...]
        m_new = jnp.maximum(m_prev, jnp.max(s, axis=-1, keepdims=True))
        p = jnp.exp2(s - jnp.tile(m_new, (1, s.shape[1] // LANES)))
        alpha = jnp.exp2(m_prev - m_new)
        acc_sc[...] = jnp.tile(alpha, (1, 2)) * acc_sc[...] + _dot(p.astype(bf16), v)
        m_sc[...] = m_new

    chunk(kc_ref[...], vc_ref[...])

    @pl.when(qi >= ncq)
    def _():
        for c in range(kl_ref.shape[0] // ck):
            chunk(kl_ref[c * ck:(c + 1) * ck, :], vl_ref[c * ck:(c + 1) * ck, :])

    acc = acc_sc[...]
    out = acc[:, :ATT_HD] / acc[:, ATT_HD:]
    for g in range(ATT_G):
        o_ref[:, g * ATT_HD:(g + 1) * ATT_HD] = out[g * tq:(g + 1) * tq, :].astype(bf16)


def _flash(q, k, v1, lay):
    R = q.shape[0]
    B, T, Tc = lay.B, lay.T, lay.Tc
    tq = lay.tm
    ncq, nlq = Tc // tq, T // tq
    gw = ATT_G * ATT_HD
    ck = min(1024, T)
    assert T % ck == 0
    kc, kl = k[:B * Tc], k[B * Tc:]
    vc, vl = v1[:B * Tc], v1[B * Tc:]

    def q_map(b, kv, qi):
        return (jnp.where(qi < ncq, b * ncq + qi, B * ncq + b * nlq + (qi - ncq)), kv)

    kv_map = lambda b, kv, qi: (b, kv)
    body = functools.partial(_flash_body, tq=tq, ncq=ncq, ck=ck)
    return pl.pallas_call(
        body,
        out_shape=jax.ShapeDtypeStruct((R, ATT_H * ATT_HD), bf16),
        grid=(B, ATT_KVH, ncq + nlq),
        in_specs=[pl.BlockSpec((tq, gw), q_map),
                  pl.BlockSpec((Tc, ATT_HD), kv_map), pl.BlockSpec((Tc, 2 * ATT_HD), kv_map),
                  pl.BlockSpec((T, ATT_HD), kv_map), pl.BlockSpec((T, 2 * ATT_HD), kv_map)],
        out_specs=pl.BlockSpec((tq, gw), q_map),
        scratch_shapes=[pltpu.VMEM((ATT_G * tq, LANES), f32), pltpu.VMEM((ATT_G * tq, 2 * ATT_HD), f32)],
        compiler_params=_cparams("parallel", "parallel", "parallel"),
        name="flash_gqa",
    )(q, kc, vc, kl, vl)


def _outproj_body(a_ref, x_ref, m_ref, w_ref, o_ref):
    o_ref[...] = x_ref[...] + m_ref[0][2:3, :] * _dot(a_ref[...], w_ref[...])


def _outproj(a, xs, mods_l, w, lay):
    R, D = xs.shape
    tm = lay.tmf
    K = a.shape[1]
    return pl.pallas_call(
        _outproj_body,
        out_shape=jax.ShapeDtypeStruct((R, D), f32),
        grid=(R // tm,),
        in_specs=[pl.BlockSpec((tm, K), lambda i: (i, 0)),
                  pl.BlockSpec((tm, D), lambda i: (i, 0)),
                  pl.BlockSpec((1, 6, D), lambda i: (lay.mod_row(i * tm), 0, 0)),
                  _const_spec(w.shape)],
        out_specs=pl.BlockSpec((tm, D), lambda i: (i, 0)),
        compiler_params=_cparams("parallel"),
        name="att_outproj",
    )(a, xs, mods_l, w)


REC_MAIN = 2 * DN_H * DN_DK + 2 * DN_H * DN_DV + 2 * GLA_H * GLA_DK + 2 * GLA_H * GLA_DV
REC_QKV = 2 * DN_H * DN_DK + DN_H * DN_DV
OFF_DZ = REC_QKV
OFF_GQ = OFF_DZ + DN_H * DN_DV
OFF_GK = OFF_GQ + GLA_H * GLA_DK
OFF_GV = OFF_GK + GLA_H * GLA_DK
OFF_GR = OFF_GV + GLA_H * GLA_DV
SM_DA = 0
SM_DB = 2 * DN_H
SM_GG = 4 * DN_H


def _recin_body(x_ref, m_ref, wm_ref, ws_ref, ym_ref, ys_ref):
    h = _normmod(x_ref[...], m_ref[0], 0, 1).astype(bf16)
    ym_ref[...] = _dot(h, wm_ref[...])
    ys_ref[...] = _dot(h, ws_ref[...])


def _recin(xs, mods_l, wm, ws, lay):
    R, D = xs.shape
    tm = lay.tmf
    return pl.pallas_call(
        _recin_body,
        out_shape=(jax.ShapeDtypeStruct((R, REC_MAIN), f32), jax.ShapeDtypeStruct((R, LANES), f32)),
        grid=(R // tm,),
        in_specs=[pl.BlockSpec((tm, D), lambda i: (i, 0)),
                  pl.BlockSpec((1, 6, D), lambda i: (lay.mod_row(i * tm), 0, 0)),
                  _const_spec(wm.shape), _const_spec(ws.shape)],
        out_specs=(pl.BlockSpec((tm, REC_MAIN), lambda i: (i, 0)), pl.BlockSpec((tm, LANES), lambda i: (i, 0))),
        compiler_params=_cparams("parallel"),
        name="rec_inproj",
    )(xs, mods_l, wm, ws)


def _recprep_body(yp_ref, y_ref, yn_ref, ys_ref, cw_ref, av_ref, dtb_ref, w2h_ref, w2l_ref, b2_ref,
                  q_ref, k_ref, v_ref, g_ref, la_ref, e_sc, *, lay, halo):
    tm = lay.tm
    row0 = pl.program_id(0) * tm
    e_sc[0:halo, :] = yp_ref[...]
    e_sc[halo:halo + tm, :] = y_ref[...]
    e_sc[halo + tm:halo + tm + halo, :] = yn_ref[...]
    rowi = lax.broadcasted_iota(jnp.int32, (tm, 1), 0)
    kill_prev = jnp.logical_and(rowi == 0, lay.is_seq_start(row0))
    kill_next = jnp.logical_and(rowi == tm - 1, jnp.logical_or(lay.is_seq_start(row0 + tm), row0 + tm == lay.R))
    cw = cw_ref[...]
    gm1 = jnp.where(kill_prev, 0.0, e_sc[pl.ds(halo - 1, tm), :])
    g0 = e_sc[pl.ds(halo, tm), :]
    gp1 = jnp.where(kill_next, 0.0, e_sc[pl.ds(halo + 1, tm), :])
    s = _silu(gm1 * cw[0:1, :] + g0 * cw[1:2, :] + gp1 * cw[2:3, :])
    nqk = DN_H * DN_DK
    for h in range(DN_H):
        z = s[:, h * DN_DK:(h + 1) * DN_DK]
        q_ref[:, h * DN_DK:(h + 1) * DN_DK] = z * lax.rsqrt(jnp.sum(z * z, axis=-1, keepdims=True) + EPS) * (DN_DK ** -0.5)
        z = s[:, nqk + h * DN_DK:nqk + (h + 1) * DN_DK]
        k_ref[:, h * DN_DK:(h + 1) * DN_DK] = z * lax.rsqrt(jnp.sum(z * z, axis=-1, keepdims=True) + EPS)
    v_ref[...] = s[:, 2 * nqk:]
    z = ys_ref[...]
    lane = lax.broadcasted_iota(jnp.int32, z.shape, 1)
    g = -jnp.exp(av_ref[...]) * jax.nn.softplus(z + dtb_ref[...])
    beta = jax.nn.sigmoid(z)
    g_ref[...] = jnp.where(lane < SM_DB, g, jnp.where(lane < SM_GG, beta, 0.0))
    zh, zl = _split2(z)
    for d in range(2):
        la = _dot(zh, w2h_ref[d]) + _dot(zl, w2h_ref[d]) + _dot(zh, w2l_ref[d]) + b2_ref[d]
        la_ref[d] = jax.nn.log_sigmoid(la) / GLA_TAU


def _recprep(ym, ys, cw, av, dtb, w2h, w2l, b2, lay):
    R = ym.shape[0]
    tm = lay.tm
    halo = 8
    nh = tm // halo
    nhb = R // halo
    nd = DN_H * DN_DK
    body = functools.partial(_recprep_body, lay=lay, halo=halo)
    return pl.pallas_call(
        body,
        out_shape=(jax.ShapeDtypeStruct((R, nd), f32), jax.ShapeDtypeStruct((R, nd), f32),
                   jax.ShapeDtypeStruct((R, DN_H * DN_DV), f32), jax.ShapeDtypeStruct((R, LANES), f32),
                   jax.ShapeDtypeStruct((2, R, GLA_H * GLA_DK), f32)),
        grid=(R // tm,),
        in_specs=[pl.BlockSpec((halo, REC_QKV), lambda i: (jnp.maximum(i * nh - 1, 0), 0)),
                  pl.BlockSpec((tm, REC_QKV), lambda i: (i, 0)),
                  pl.BlockSpec((halo, REC_QKV), lambda i: (jnp.minimum((i + 1) * nh, nhb - 1), 0)),
                  pl.BlockSpec((tm, LANES), lambda i: (i, 0)),
                  _const_spec(cw.shape), _const_spec(av.shape), _const_spec(dtb.shape),
                  _const_spec(w2h.shape), _const_spec(w2l.shape), _const_spec(b2.shape)],
        out_specs=(pl.BlockSpec((tm, nd), lambda i: (i, 0)), pl.BlockSpec((tm, nd), lambda i: (i, 0)),
                   pl.BlockSpec((tm, DN_H * DN_DV), lambda i: (i, 0)), pl.BlockSpec((tm, LANES), lambda i: (i, 0)),
                   pl.BlockSpec((2, tm, GLA_H * GLA_DK), lambda i: (0, i, 0))),
        scratch_shapes=[pltpu.VMEM((tm + 2 * halo, REC_QKV), f32)],
        compiler_params=_cparams("parallel"),
        name="rec_prep",
    )(ym, ym, ym, ys, cw, av, dtb, w2h, w2l, b2)


def _tri_masks(rev, width):
    ri = lax.broadcasted_iota(jnp.int32, (CHUNK, width), 0)
    ci = lax.broadcasted_iota(jnp.int32, (CHUNK, width), 1) % CHUNK
    if rev:
        return ri <= ci, ri < ci, ri >= ci
    return ri >= ci, ri > ci, ri <= ci


def _block_diag(x, nblk):
    w = x.shape[1]
    t = jnp.concatenate([x] * nblk, axis=0)
    rb = lax.broadcasted_iota(jnp.int32, t.shape, 0) // CHUNK
    cb = lax.broadcasted_iota(jnp.int32, t.shape, 1) // (w // nblk)
    return jnp.where(rb == cb, t, jnp.zeros_like(t))


def _delta_dir(rev, q_ref, k_ref, v_ref, g_ref, o_ref, s_sc, d):
    H = DN_H
    gt = g_ref[...]
    q_all = q_ref[...]
    k_all = k_ref[...]
    v_all = v_ref[...]
    incl, strict, incl_t = _tri_masks(rev, CHUNK)
    incl4, strict4, incl_t4 = _tri_masks(rev, H * CHUNK)
    mb = incl.astype(bf16)
    gcol = [gt[:, SM_DA + d * H + h:SM_DA + d * H + h + 1] for h in range(H)]
    bcol = [gt[:, SM_DB + d * H + h:SM_DB + d * H + h + 1] for h in range(H)]
    g128 = jnp.concatenate([jnp.broadcast_to(c, (CHUNK, DN_DK)) for c in gcol], axis=1)
    b128 = jnp.concatenate([jnp.broadcast_to(c, (CHUNK, DN_DK)) for c in bcol], axis=1)
    g64 = jnp.concatenate([jnp.broadcast_to(c, (CHUNK, CHUNK)) for c in gcol], axis=1)
    b64 = jnp.concatenate([jnp.broadcast_to(c, (CHUNK, CHUNK)) for c in bcol], axis=1)
    hi, lo = _split2(g128)
    G = _dot(jnp.concatenate([mb, mb], axis=1), jnp.concatenate([hi, lo], axis=0))
    hi6, lo6 = _split2(g64)
    zero = jnp.zeros_like(hi6)
    rhs = jnp.concatenate([hi6, lo6, jnp.where(incl_t4, hi6, zero), jnp.where(incl_t4, lo6, zero)], axis=0)
    neg1 = jnp.full((CHUNK, CHUNK), -1.0, bf16)
    diff = _dot(jnp.concatenate([mb, mb, neg1, neg1], axis=1), rhs)
    decay = jnp.where(incl4, jnp.exp(diff), 0.0)
    k_bf = k_all.astype(bf16)
    kq = jnp.concatenate([k_bf, q_all.astype(bf16)], axis=0)
    kkqk = _dot_nt(kq, _block_diag(k_bf, H))
    kk = kkqk[:CHUNK]
    qk = kkqk[CHUNK:]
    a = jnp.where(strict4, b64 * kk * decay, 0.0)
    ri = lax.broadcasted_iota(jnp.int32, (CHUNK, H * CHUNK), 0)
    ci = lax.broadcasted_iota(jnp.int32, (CHUNK, H * CHUNK), 1) % CHUNK
    p = jnp.where(ri == ci, 1.0, 0.0) - jnp.where(ri // 2 == ci // 2, a, 0.0)
    m = 2
    while m < CHUNK:
        a_o = jnp.where(jnp.logical_and(ri // (2 * m) == ci // (2 * m), ri // m != ci // m), a, 0.0)
        x = _dot(p.astype(bf16), _block_diag(a_o.astype(bf16), H))
        p = p - _dot(x.astype(bf16), _block_diag(p.astype(bf16), H))
        m *= 2
    tb = p.astype(bf16)
    intra = (qk * decay).astype(bf16)
    last = 0 if rev else CHUNK - 1
    g_last = G[last:last + 1, :]
    e_g = jnp.exp(G)
    e_last = jnp.exp(g_last)
    vbeta = v_all * b128
    kbg = k_all * b128 * e_g
    qd = q_all * e_g
    kd = k_all * jnp.exp(g_last - G)
    for h in range(H):
        sl = slice(h * DN_DK, (h + 1) * DN_DK)
        sc = slice(h * CHUNK, (h + 1) * CHUNK)
        uw = _dot(tb[:, sc], jnp.concatenate([vbeta[:, sl], kbg[:, sl]], axis=1).astype(bf16))
        u = uw[:, :DN_DV]
        w = uw[:, DN_DV:]
        S = s_sc[d, h]
        r1 = _dot(jnp.concatenate([w, qd[:, sl]], axis=0).astype(bf16), S.astype(bf16))
        vnb = (u - r1[:CHUNK]).astype(bf16)
        o_ref[:, sl] = r1[CHUNK:] + _dot(intra[:, sc], vnb)
        s_sc[d, h] = S * e_last[:, sl] + _dot_tn(kd[:, sl].astype(bf16), vnb)


def _delta_body(qf, kf, vf, gf, qb, kb, vb, gb, of_ref, ob_ref, s_sc):
    @pl.when(pl.program_id(1) == 0)
    def _():
        s_sc[...] = jnp.zeros_like(s_sc)

    _delta_dir(False, qf, kf, vf, gf, of_ref, s_sc, 0)
    _delta_dir(True, qb, kb, vb, gb, ob_ref, s_sc, 1)


def _chunk_maps(lay):
    B = lay.B
    ncc, ncl = lay.Tc // CHUNK, lay.T // CHUNK

    def fwd(b, n):
        return jnp.where(n < ncc, b * ncc + n, B * ncc + b * ncl + (n - ncc))

    def bwd(b, n):
        return jnp.where(n < ncc, b * ncc + (ncc - 1 - n), B * ncc + b * ncl + (ncl - 1 - (n - ncc)))

    return fwd, bwd, ncc + ncl


def _delta(q, k, v, gates, lay):
    R = q.shape[0]
    fwd, bwd, nstep = _chunk_maps(lay)
    wq = DN_H * DN_DK
    wv = DN_H * DN_DV

    def spec(w, m):
        return pl.BlockSpec((CHUNK, w), lambda b, n: (m(b, n), 0))

    return pl.pallas_call(
        _delta_body,
        out_shape=(jax.ShapeDtypeStruct((R, wv), f32), jax.ShapeDtypeStruct((R, wv), f32)),
        grid=(lay.B, nstep),
        in_specs=[spec(wq, fwd), spec(wq, fwd), spec(wv, fwd), spec(LANES, fwd),
                  spec(wq, bwd), spec(wq, bwd), spec(wv, bwd), spec(LANES, bwd)],
        out_specs=(spec(wv, fwd), spec(wv, bwd)),
        scratch_shapes=[pltpu.VMEM((2, DN_H, DN_DK, DN_DV), f32)],
        compiler_params=_cparams("parallel", "arbitrary"),
        name="delta_scan",
    )(q, k, v, gates, q, k, v, gates)


def _gla_dir(rev, q_ref, k_ref, v_ref, la_ref, o_ref, st_sc, d):
    H = GLA_H
    la = la_ref[0]
    incl4 = _tri_masks(rev, H * CHUNK)[0]
    mb = _tri_masks(rev, CHUNK)[0].astype(bf16)
    hi, lo = _split2(la)
    b = _dot(jnp.concatenate([mb, mb], axis=1), jnp.concatenate([hi, lo], axis=0))
    mid = CHUNK - 1 - CHUNK // 2 if rev else CHUNK // 2
    last = 0 if rev else CHUNK - 1
    b_mid = b[mid:mid + 1, :]
    b_last = b[last:last + 1, :]
    q = q_ref[...] * (GLA_DK ** -0.5)
    k = k_ref[...]
    v_bf = v_ref[...].astype(bf16)
    qe = (q * jnp.exp(b - b_mid)).astype(bf16)
    ke = (k * jnp.exp(b_mid - b)).astype(bf16)
    att = jnp.where(incl4, _dot_nt(qe, _block_diag(ke, H)), 0.0).astype(bf16)
    o_intra = _dot(att, _block_diag(v_bf, H))
    qi = (q * jnp.exp(b)).astype(bf16)
    ks = (k * jnp.exp(b_last - b)).astype(bf16)
    st = st_sc[d]
    st_bf = st.astype(bf16)
    kv = []
    for h in range(H):
        sk = slice(h * GLA_DK, (h + 1) * GLA_DK)
        sv = slice(h * GLA_DV, (h + 1) * GLA_DV)
        o_ref[:, sv] = o_intra[:, sv] + _dot_nt(qi[:, sk], st_bf[:, sk])
        kv.append(_dot_tn(v_bf[:, sv], ks[:, sk]))
    st_sc[d] = st * jnp.exp(b_last) + jnp.concatenate(kv, axis=1)


def _gla_body(qf, kf, vf, lf, qb, kb, vb, lb, of_ref, ob_ref, st_sc):
    @pl.when(pl.program_id(1) == 0)
    def _():
        st_sc[...] = jnp.zeros_like(st_sc)

    _gla_dir(False, qf, kf, vf, lf, of_ref, st_sc, 0)
    _gla_dir(True, qb, kb, vb, lb, ob_ref, st_sc, 1)


def _gla(ym, la, lay):
    R = ym.shape[0]
    fwd, bwd, nstep = _chunk_maps(lay)
    wk = GLA_H * GLA_DK
    wv = GLA_H * GLA_DV

    def spec(w, m, col):
        return pl.BlockSpec((CHUNK, w), lambda b, n: (m(b, n), col))

    def la_spec(m, d):
        return pl.BlockSpec((1, CHUNK, wk), lambda b, n: (d, m(b, n), 0))

    cq, ck, cv = OFF_GQ // wk, OFF_GK // wk, OFF_GV // wv
    return pl.pallas_call(
        _gla_body,
        out_shape=(jax.ShapeDtypeStruct((R, wv), f32), jax.ShapeDtypeStruct((R, wv), f32)),
        grid=(lay.B, nstep),
        in_specs=[spec(wk, fwd, cq), spec(wk, fwd, ck), spec(wv, fwd, cv), la_spec(fwd, 0),
                  spec(wk, bwd, cq), spec(wk, bwd, ck), spec(wv, bwd, cv), la_spec(bwd, 1)],
        out_specs=(spec(wv, fwd, 0), spec(wv, bwd, 0)),
        scratch_shapes=[pltpu.VMEM((2, GLA_DV, GLA_H * GLA_DK), f32)],
        compiler_params=_cparams("parallel", "arbitrary"),
        name="gla_scan",
    )(ym, ym, ym, la, ym, ym, ym, la)


def _recout_body(df_ref, db_ref, gf_ref, gb_ref, z_ref, r_ref, x_ref, m_ref, dn_ref, gn_ref, w_ref, o_ref):
    dn = df_ref[...] + db_ref[...]
    gl = gf_ref[...] + gb_ref[...]
    z = z_ref[...]
    r = r_ref[...]
    parts = []
    for src, gate, gain, hd, nh in ((dn, z, dn_ref[...], DN_DV, DN_H), (gl, r, gn_ref[...], GLA_DV, GLA_H)):
        for h in range(nh):
            a = src[:, h * hd:(h + 1) * hd]
            a = a * lax.rsqrt(jnp.mean(a * a, axis=-1, keepdims=True) + EPS) * gain
            parts.append(a * _silu(gate[:, h * hd:(h + 1) * hd]))
    mix = jnp.concatenate(parts, axis=1).astype(bf16)
    o_ref[...] = x_ref[...] + m_ref[0][2:3, :] * _dot(mix, w_ref[...])


def _recout(dn_f, dn_b, gl_f, gl_b, ym, xs, mods_l, dn_norm, gla_norm, w, lay):
    R, D = xs.shape
    tm = lay.tmf
    wd = DN_H * DN_DV
    wg = GLA_H * GLA_DV
    row = lambda i: (i, 0)
    return pl.pallas_call(
        _recout_body,
        out_shape=jax.ShapeDtypeStruct((R, D), f32),
        grid=(R // tm,),
        in_specs=[pl.BlockSpec((tm, wd), row), pl.BlockSpec((tm, wd), row),
                  pl.BlockSpec((tm, wg), row), pl.BlockSpec((tm, wg), row),
                  pl.BlockSpec((tm, wd), lambda i: (i, OFF_DZ // wd)),
                  pl.BlockSpec((tm, wg), lambda i: (i, OFF_GR // wg)),
                  pl.BlockSpec((tm, D), row),
                  pl.BlockSpec((1, 6, D), lambda i: (lay.mod_row(i * tm), 0, 0)),
                  _const_spec(dn_norm.shape), _const_spec(gla_norm.shape), _const_spec(w.shape)],
        out_specs=pl.BlockSpec((tm, D), row),
        compiler_params=_cparams("parallel"),
        name="rec_outproj",
    )(dn_f, dn_b, gl_f, gl_b, ym, ym, xs, mods_l, dn_norm, gla_norm, w)


def _final_body(x_ref, g_ref, o_ref):
    x = x_ref[...]
    o_ref[...] = x * lax.rsqrt(jnp.mean(x * x, axis=-1, keepdims=True) + EPS) * g_ref[...]


def _final(xs, gain, lay):
    R, D = xs.shape
    tm = lay.tm
    off = lay.B * lay.Tc // tm
    n = lay.B * lay.T
    return pl.pallas_call(
        _final_body,
        out_shape=jax.ShapeDtypeStruct((n, D), f32),
        grid=(n // tm,),
        in_specs=[pl.BlockSpec((tm, D), lambda i: (i + off, 0)), _const_spec(gain.shape)],
        out_specs=pl.BlockSpec((tm, D), lambda i: (i, 0)),
        compiler_params=_cparams("parallel"),
        name="final_norm",
    )(xs, gain)


def _rec_weights(w_in):
    sizes = (DN_H * DN_DK, DN_H * DN_DK, DN_H * DN_DV, DN_H * DN_DV, 2 * DN_H, 2 * DN_H,
             GLA_H * GLA_DK, GLA_H * GLA_DK, GLA_H * GLA_DV, GLA_H * GLA_DV, 2 * GLA_RANK)
    offs = [0]
    for s in sizes:
        offs.append(offs[-1] + s)
    seg = [w_in[:, offs[i]:offs[i + 1]] for i in range(len(sizes))]
    dq, dk, dv, dz, da, db, gq, gk, gv, gr, gg = seg
    main = jnp.concatenate([dq, dk, dv, dz, gq, gk, gv, gr], axis=1).astype(bf16)
    small = jnp.concatenate([da, db, gg], axis=1)
    small = jnp.pad(small, ((0, 0), (0, LANES - small.shape[1]))).astype(bf16)
    return main, small


def _pad_lanes(v, off=0):
    v = v.reshape(1, -1)
    return jnp.pad(v, ((0, 0), (off, LANES - off - v.shape[1])))


def _gla_gate_weights(w2):
    out = jnp.zeros((2, LANES, w2.shape[2]), f32)
    for d in range(2):
        out = out.at[d, SM_GG + d * GLA_RANK:SM_GG + (d + 1) * GLA_RANK, :].set(w2[d])
    hi = out.astype(bf16)
    lo = (out - hi.astype(f32)).astype(bf16)
    return hi, lo


def _ffn_weights(w_up, conv, w_down):
    D, F2 = w_up.shape
    F = F2 // 2
    n = F // FFN_CHUNK
    wg = w_up[:, :F].astype(bf16).reshape(D, n, FFN_CHUNK).transpose(1, 0, 2)
    wv = w_up[:, F:].astype(bf16).reshape(D, n, FFN_CHUNK).transpose(1, 0, 2)
    cw = conv.reshape(3, n, FFN_CHUNK).transpose(1, 0, 2)
    wd = w_down.astype(bf16).reshape(n, FFN_CHUNK, D)
    return wg, wv, cw, wd


def _rope_tables(lay):
    T, Tc, B = lay.T, lay.Tc, lay.B
    t = jnp.arange(T, dtype=jnp.int32)
    row = (t // GRID_W).astype(f32)
    col = (t % GRID_W).astype(f32)
    inv_freq = ROPE_THETA ** (-jnp.arange(ROPE_PAIRS, dtype=f32) / ROPE_PAIRS)
    ar = row[:, None] * inv_freq
    ac = col[:, None] * inv_freq
    cos = jnp.concatenate([jnp.cos(ar), jnp.cos(ar), jnp.cos(ac), jnp.cos(ac)], axis=1)
    sin = jnp.concatenate([-jnp.sin(ar), jnp.sin(ar), -jnp.sin(ac), jnp.sin(ac)], axis=1)
    cos = jnp.concatenate([jnp.ones((B * Tc, ATT_HD), f32)] + [cos] * B, axis=0)
    sin = jnp.concatenate([jnp.zeros((B * Tc, ATT_HD), f32)] + [sin] * B, axis=0)
    return cos, sin


def kernel(x, c, ctx, c_ctx, mod_w, mod_b, rec_w_in, rec_conv, dn_a_log, dn_dt_bias, dn_norm, gla_w2, gla_b2,
           gla_norm, rec_w_out, att_w_qkv, att_q_norm, att_k_norm, att_w_out, ffn_w_up, ffn_conv, ffn_w_down,
           final_norm):
    B, T, D = x.shape
    Tc = ctx.shape[1]
    depth = mod_w.shape[0]
    lay = _Layout(B, T, Tc)
    assert B + 1 <= MOD_ROWS
    xs = jnp.concatenate([ctx.reshape(B * Tc, D), x.reshape(B * T, D)], axis=0)
    cond = jnp.concatenate([c, c_ctx[None], jnp.zeros((MOD_ROWS - B - 1, D), f32)], axis=0)
    mods = _mods(cond, mod_w, mod_b)
    cos, sin = _rope_tables(lay)

    for i in range(depth):
        ml = mods[i]
        if i % 2 == 0:
            e = i // 2
            wm, ws = _rec_weights(rec_w_in[e])
            ym, ys = _recin(xs, ml, wm, ws, lay)
            w2h, w2l = _gla_gate_weights(gla_w2[e])
            q, k, v, gates, la = _recprep(
                ym, ys, rec_conv[e], _pad_lanes(dn_a_log[e], SM_DA), _pad_lanes(dn_dt_bias[e], SM_DA),
                w2h, w2l, gla_b2[e].reshape(2, 1, -1), lay)
            dn_f, dn_b = _delta(q, k, v, gates, lay)
            gl_f, gl_b = _gla(ym, la, lay)
            xs = _recout(dn_f, dn_b, gl_f, gl_b, ym, xs, ml, dn_norm[e].reshape(1, -1), gla_norm[e].reshape(1, -1),
                         rec_w_out[e].astype(bf16), lay)
        else:
            o = i // 2
            q, k, v = _qkv(xs, ml, att_w_qkv[o].astype(bf16), att_q_norm[o].reshape(1, -1),
                           att_k_norm[o].reshape(1, -1), cos, sin, lay)
            a = _flash(q, k, v, lay)
            xs = _outproj(a, xs, ml, att_w_out[o].astype(bf16), lay)
        wg, wv, cw, wd = _ffn_weights(ffn_w_up[i], ffn_conv[i], ffn_w_down[i])
        xs = _ffn(xs, ml, wg, wv, cw, wd, lay)

    out = _final(xs, final_norm.reshape(1, -1), lay)
    return out.reshape(B, T, D)
```

```python
import functools
import math

import jax
import jax.numpy as jnp
from jax import lax
from jax.experimental import pallas as pl
from jax.experimental.pallas import tpu as pltpu

f32 = jnp.float32
bf16 = jnp.bfloat16

EPS = 1e-6
GRID_W = 64
DN_H = 4
DN_DK = 128
DN_DV = 128
GLA_H = 4
GLA_DK = 64
GLA_DV = 128
GLA_RANK = 16
GLA_TAU = 16.0
CHUNK = 64
ATT_H = 8
ATT_KVH = 2
ATT_G = ATT_H // ATT_KVH
ATT_HD = 128
ROPE_THETA = 10000.0
ROPE_PAIRS = ATT_HD // 4
LOG2E = math.log2(math.e)
FFN_CHUNK = 256
MOD_ROWS = 8
LANES = 128
VMEM_LIMIT = 56 * 1024 * 1024


def _cparams(*sem):
    return pltpu.CompilerParams(dimension_semantics=sem, vmem_limit_bytes=VMEM_LIMIT)


def _const_spec(shape):
    nd = len(shape)
    return pl.BlockSpec(shape, lambda *_: (0,) * nd, pipeline_mode=pl.Buffered(1))


class _Layout:
    def __init__(self, B, T, Tc):
        self.B, self.T, self.Tc = B, T, Tc
        self.R = B * Tc + B * T
        self.tm = min(256, Tc)
        assert Tc % self.tm == 0 and T % self.tm == 0 and T % CHUNK == 0 and Tc % CHUNK == 0
        self.tmf = min(512, B * Tc)
        assert (B * Tc) % self.tmf == 0 and T % self.tmf == 0
        self.seq_starts = [b * Tc for b in range(B)] + [B * Tc + b * T for b in range(B)]

    def mod_row(self, row0):
        return jnp.where(row0 < self.B * self.Tc, self.B, (row0 - self.B * self.Tc) // self.T)

    def is_seq_start(self, row0):
        return jnp.where(row0 < self.B * self.Tc, row0 % self.Tc == 0, (row0 - self.B * self.Tc) % self.T == 0)

    def edge_masks(self, rows):
        first = rows == self.seq_starts[0]
        last = rows == self.R - 1
        for s in self.seq_starts[1:]:
            first = jnp.logical_or(first, rows == s)
            last = jnp.logical_or(last, rows == s - 1)
        return first, last


def _normmod(x, m, shift_i, scale_i):
    ms = jnp.mean(x * x, axis=-1, keepdims=True)
    return (x * lax.rsqrt(ms + EPS)) * (1.0 + m[scale_i:scale_i + 1, :]) + m[shift_i:shift_i + 1, :]


def _silu(x):
    return x * jax.nn.sigmoid(x)


def _split2(x):
    hi = x.astype(bf16)
    lo = (x - hi.astype(f32)).astype(bf16)
    return hi, lo


def _dot(a, b):
    return jnp.dot(a, b, preferred_element_type=f32)


def _dot_nt(a, b):
    return lax.dot_general(a, b, (((1,), (1,)), ((), ())), preferred_element_type=f32)


def _dot_tn(a, b):
    return lax.dot_general(a, b, (((0,), (0,)), ((), ())), preferred_element_type=f32)


def _mod_body(c_ref, w_ref, b_ref, o_ref):
    a = _silu(c_ref[...]).astype(bf16)
    o_ref[0] = _dot(a, w_ref[0].astype(bf16)) + b_ref[0]


def _mods(cond, mod_w, mod_b):
    depth, D, D6 = mod_w.shape
    tn = 1536
    out = pl.pallas_call(
        _mod_body,
        out_shape=jax.ShapeDtypeStruct((depth, MOD_ROWS, D6), f32),
        grid=(depth, D6 // tn),
        in_specs=[pl.BlockSpec((MOD_ROWS, D), lambda l, j: (0, 0)),
                  pl.BlockSpec((1, D, tn), lambda l, j: (l, 0, j)),
                  pl.BlockSpec((1, 1, tn), lambda l, j: (l, 0, j))],
        out_specs=pl.BlockSpec((1, MOD_ROWS, tn), lambda l, j: (l, 0, j)),
        compiler_params=_cparams("parallel", "parallel"),
        name="mods",
    )(cond, mod_w, mod_b.reshape(depth, 1, D6))
    return out.reshape(depth, MOD_ROWS, 6, D)


def _ffn_body(xp_ref, x_ref, xn_ref, m_ref, wg_ref, wv_ref, cw_ref, wd_ref, o_ref, h_sc, g_sc, acc_sc, *, lay, halo, ncf):
    tm = lay.tmf
    row0 = pl.program_id(0) * tm
    m = m_ref[0]
    x = x_ref[...]
    xe = jnp.concatenate([xp_ref[...], x, xn_ref[...]], axis=0)
    h_sc[...] = _normmod(xe, m, 3, 4).astype(bf16)
    kill_prev, kill_next = lay.edge_masks(row0 + lax.broadcasted_iota(jnp.int32, (tm, 1), 0))
    acc_sc[...] = jnp.zeros_like(acc_sc)

    def step(c, carry):
        g_sc[...] = _dot(h_sc[...], wg_ref[c])
        val = _dot(h_sc[pl.ds(halo, tm), :], wv_ref[c])
        cw = cw_ref[c]
        gm1 = jnp.where(kill_prev, 0.0, g_sc[pl.ds(halo - 1, tm), :])
        g0 = g_sc[pl.ds(halo, tm), :]
        gp1 = jnp.where(kill_next, 0.0, g_sc[pl.ds(halo + 1, tm), :])
        gate = gm1 * cw[0:1, :] + g0 * cw[1:2, :] + gp1 * cw[2:3, :]
        act = (_silu(gate) * val).astype(bf16)
        acc_sc[...] += _dot(act, wd_ref[c])
        return carry

    lax.fori_loop(0, ncf, step, 0, unroll=True)
    o_ref[...] = x + m[5:6, :] * acc_sc[...]


def _ffn(xs, mods_l, wg, wv, cw, wd, lay):
    R, D = xs.shape
    tm = lay.tmf
    halo = 16
    ncf = wg.shape[0]
    cf = wg.shape[2]
    nh = tm // halo
    nhb = R // halo
    body = functools.partial(_ffn_body, lay=lay, halo=halo, ncf=ncf)
    return pl.pallas_call(
        body,
        out_shape=jax.ShapeDtypeStruct((R, D), f32),
        grid=(R // tm,),
        in_specs=[pl.BlockSpec((halo, D), lambda i: (jnp.maximum(i * nh - 1, 0), 0)),
                  pl.BlockSpec((tm, D), lambda i: (i, 0)),
                  pl.BlockSpec((halo, D), lambda i: (jnp.minimum((i + 1) * nh, nhb - 1), 0)),
                  pl.BlockSpec((1, 6, D), lambda i: (lay.mod_row(i * tm), 0, 0)),
                  _const_spec(wg.shape), _const_spec(wv.shape), _const_spec(cw.shape), _const_spec(wd.shape)],
        out_specs=pl.BlockSpec((tm, D), lambda i: (i, 0)),
        scratch_shapes=[pltpu.VMEM((tm + 2 * halo, D), bf16),
                        pltpu.VMEM((tm + 2 * halo, cf), f32),
                        pltpu.VMEM((tm, D), f32)],
        compiler_params=_cparams("parallel"),
        name="conv_ffn",
    )(xs, xs, xs, mods_l, wg, wv, cw, wd)


def _qkv_body(x_ref, m_ref, w_ref, qn_ref, kn_ref, cos_ref, sin_ref, q_ref, k_ref, v_ref):
    h = _normmod(x_ref[...], m_ref[0], 0, 1).astype(bf16)
    y = _dot(h, w_ref[...])
    cos = cos_ref[...]
    sin = sin_ref[...]
    lane = lax.broadcasted_iota(jnp.int32, cos.shape, 1)
    first = (lane % (2 * ROPE_PAIRS)) < ROPE_PAIRS

    def norm_rope(z, gain):
        z = z * lax.rsqrt(jnp.mean(z * z, axis=-1, keepdims=True) + EPS) * gain
        partner = jnp.where(first, pltpu.roll(z, LANES - ROPE_PAIRS, axis=1), pltpu.roll(z, ROPE_PAIRS, axis=1))
        return z * cos + partner * sin

    nq = ATT_H * ATT_HD
    for hh in range(ATT_H):
        z = norm_rope(y[:, hh * ATT_HD:(hh + 1) * ATT_HD], qn_ref[...])
        q_ref[:, hh * ATT_HD:(hh + 1) * ATT_HD] = (z * (ATT_HD ** -0.5 * LOG2E)).astype(bf16)
    for kk in range(ATT_KVH):
        z = norm_rope(y[:, nq + kk * ATT_HD:nq + (kk + 1) * ATT_HD], kn_ref[...])
        k_ref[:, kk * ATT_HD:(kk + 1) * ATT_HD] = z.astype(bf16)
    nk = nq + ATT_KVH * ATT_HD
    ones = jnp.ones((y.shape[0], ATT_HD), bf16)
    for kk in range(ATT_KVH):
        v_ref[:, 2 * kk * ATT_HD:(2 * kk + 1) * ATT_HD] = y[:, nk + kk * ATT_HD:nk + (kk + 1) * ATT_HD].astype(bf16)
        v_ref[:, (2 * kk + 1) * ATT_HD:(2 * kk + 2) * ATT_HD] = ones


def _qkv(xs, mods_l, w, qn, kn, cos, sin, lay):
    R, D = xs.shape
    tm = lay.tmf
    nq = ATT_H * ATT_HD
    nkv = ATT_KVH * ATT_HD
    return pl.pallas_call(
        _qkv_body,
        out_shape=(jax.ShapeDtypeStruct((R, nq), bf16), jax.ShapeDtypeStruct((R, nkv), bf16),
                   jax.ShapeDtypeStruct((R, 2 * nkv), bf16)),
        grid=(R // tm,),
        in_specs=[pl.BlockSpec((tm, D), lambda i: (i, 0)),
                  pl.BlockSpec((1, 6, D), lambda i: (lay.mod_row(i * tm), 0, 0)),
                  _const_spec(w.shape), _const_spec(qn.shape), _const_spec(kn.shape),
                  pl.BlockSpec((tm, ATT_HD), lambda i: (i, 0)),
                  pl.BlockSpec((tm, ATT_HD), lambda i: (i, 0))],
        out_specs=(pl.BlockSpec((tm, nq), lambda i: (i, 0)), pl.BlockSpec((tm, nkv), lambda i: (i, 0)),
                   pl.BlockSpec((tm, 2 * nkv), lambda i: (i, 0))),
        compiler_params=_cparams("parallel"),
        name="qkv_rope",
    )(xs, mods_l, w, qn, kn, cos, sin)


def _flash_body(q_ref, kc_ref, vc_ref, kl_ref, vl_ref, o_ref, m_sc, acc_sc, *, tq, ncq, ck):
    qi = pl.program_id(2)
    q = q_ref[...]
    q4 = jnp.concatenate([q[:, g * ATT_HD:(g + 1) * ATT_HD] for g in range(ATT_G)], axis=0)
    m_sc[...] = jnp.full_like(m_sc, -jnp.inf)
    acc_sc[...] = jnp.zeros_like(acc_sc)

    def chunk(s, v):
        m_prev = m_sc[...]
        m_new = jnp.maximum(m_prev, jnp.max(s, axis=-1, keepdims=True))
        p = jnp.exp2(s - jnp.tile(m_new, (1, s.shape[1] // LANES)))
        alpha = jnp.exp2(m_prev - m_new)
        acc_sc[...] = jnp.tile(alpha, (1, 2)) * acc_sc[...] + _dot(p.astype(bf16), v)
        m_sc[...] = m_new

    chunk(_dot_nt(q4, kc_ref[...]), vc_ref[...])

    @pl.when(qi >= ncq)
    def _():
        n = kl_ref.shape[0] // ck
        s_next = _dot_nt(q4, kl_ref[0:ck, :])
        for c in range(n):
            s = s_next
            if c + 1 < n:
                s_next = _dot_nt(q4, kl_ref[(c + 1) * ck:(c + 2) * ck, :])
            chunk(s, vl_ref[c * ck:(c + 1) * ck, :])

    acc = acc_sc[...]
    out = acc[:, :ATT_HD] / acc[:, ATT_HD:]
    for g in range(ATT_G):
        o_ref[:, g * ATT_HD:(g + 1) * ATT_HD] = out[g * tq:(g + 1) * tq, :].astype(bf16)


def _flash(q, k, v1, lay):
    R = q.shape[0]
    B, T, Tc = lay.B, lay.T, lay.Tc
    tq = lay.tm
    ncq, nlq = Tc // tq, T // tq
    gw = ATT_G * ATT_HD
    ck = min(1024, T)
    assert T % ck == 0
    kc, kl = k[:B * Tc], k[B * Tc:]
    vc, vl = v1[:B * Tc], v1[B * Tc:]

    def q_map(b, kv, qi):
        return (jnp.where(qi < ncq, b * ncq + qi, B * ncq + b * nlq + (qi - ncq)), kv)

    kv_map = lambda b, kv, qi: (b, kv)
    body = functools.partial(_flash_body, tq=tq, ncq=ncq, ck=ck)
    return pl.pallas_call(
        body,
        out_shape=jax.ShapeDtypeStruct((R, ATT_H * ATT_HD), bf16),
        grid=(B, ATT_KVH, ncq + nlq),
        in_specs=[pl.BlockSpec((tq, gw), q_map),
                  pl.BlockSpec((Tc, ATT_HD), kv_map), pl.BlockSpec((Tc, 2 * ATT_HD), kv_map),
                  pl.BlockSpec((T, ATT_HD), kv_map), pl.BlockSpec((T, 2 * ATT_HD), kv_map)],
        out_specs=pl.BlockSpec((tq, gw), q_map),
        scratch_shapes=[pltpu.VMEM((ATT_G * tq, LANES), f32), pltpu.VMEM((ATT_G * tq, 2 * ATT_HD), f32)],
        compiler_params=_cparams("parallel", "parallel", "parallel"),
        name="flash_gqa",
    )(q, kc, vc, kl, vl)


def _outproj_body(a_ref, x_ref, m_ref, w_ref, o_ref):
    o_ref[...] = x_ref[...] + m_ref[0][2:3, :] * _dot(a_ref[...], w_ref[...])


def _outproj(a, xs, mods_l, w, lay):
    R, D = xs.shape
    tm = lay.tmf
    K = a.shape[1]
    return pl.pallas_call(
        _outproj_body,
        out_shape=jax.ShapeDtypeStruct((R, D), f32),
        grid=(R // tm,),
        in_specs=[pl.BlockSpec((tm, K), lambda i: (i, 0)),
                  pl.BlockSpec((tm, D), lambda i: (i, 0)),
                  pl.BlockSpec((1, 6, D), lambda i: (lay.mod_row(i * tm), 0, 0)),
                  _const_spec(w.shape)],
        out_specs=pl.BlockSpec((tm, D), lambda i: (i, 0)),
        compiler_params=_cparams("parallel"),
        name="att_outproj",
    )(a, xs, mods_l, w)


REC_MAIN = 2 * DN_H * DN_DK + 2 * DN_H * DN_DV + 2 * GLA_H * GLA_DK + 2 * GLA_H * GLA_DV
REC_QKV = 2 * DN_H * DN_DK + DN_H * DN_DV
OFF_DZ = REC_QKV
OFF_GQ = OFF_DZ + DN_H * DN_DV
OFF_GK = OFF_GQ + GLA_H * GLA_DK
OFF_GV = OFF_GK + GLA_H * GLA_DK
OFF_GR = OFF_GV + GLA_H * GLA_DV
SM_DA = 0
SM_DB = 2 * DN_H
SM_GG = 4 * DN_H


def _recin_body(x_ref, m_ref, wm_ref, ws_ref, ym_ref, ys_ref):
    h = _normmod(x_ref[...], m_ref[0], 0, 1).astype(bf16)
    ym_ref[...] = _dot(h, wm_ref[...])
    ys_ref[...] = _dot(h, ws_ref[...])


def _recin(xs, mods_l, wm, ws, lay):
    R, D = xs.shape
    tm = lay.tmf
    return pl.pallas_call(
        _recin_body,
        out_shape=(jax.ShapeDtypeStruct((R, REC_MAIN), f32), jax.ShapeDtypeStruct((R, LANES), f32)),
        grid=(R // tm,),
        in_specs=[pl.BlockSpec((tm, D), lambda i: (i, 0)),
                  pl.BlockSpec((1, 6, D), lambda i: (lay.mod_row(i * tm), 0, 0)),
                  _const_spec(wm.shape), _const_spec(ws.shape)],
        out_specs=(pl.BlockSpec((tm, REC_MAIN), lambda i: (i, 0)), pl.BlockSpec((tm, LANES), lambda i: (i, 0))),
        compiler_params=_cparams("parallel"),
        name="rec_inproj",
    )(xs, mods_l, wm, ws)


def _recprep_body(yp_ref, y_ref, yn_ref, ys_ref, cw_ref, av_ref, dtb_ref, w2h_ref, w2l_ref, b2_ref,
                  q_ref, k_ref, v_ref, g_ref, la_ref, e_sc, *, lay, halo):
    tm = lay.tm
    row0 = pl.program_id(0) * tm
    e_sc[0:halo, :] = yp_ref[...]
    e_sc[halo:halo + tm, :] = y_ref[...]
    e_sc[halo + tm:halo + tm + halo, :] = yn_ref[...]
    rowi = lax.broadcasted_iota(jnp.int32, (tm, 1), 0)
    kill_prev = jnp.logical_and(rowi == 0, lay.is_seq_start(row0))
    kill_next = jnp.logical_and(rowi == tm - 1, jnp.logical_or(lay.is_seq_start(row0 + tm), row0 + tm == lay.R))
    cw = cw_ref[...]
    gm1 = jnp.where(kill_prev, 0.0, e_sc[pl.ds(halo - 1, tm), :])
    g0 = e_sc[pl.ds(halo, tm), :]
    gp1 = jnp.where(kill_next, 0.0, e_sc[pl.ds(halo + 1, tm), :])
    s = _silu(gm1 * cw[0:1, :] + g0 * cw[1:2, :] + gp1 * cw[2:3, :])
    nqk = DN_H * DN_DK
    for h in range(DN_H):
        z = s[:, h * DN_DK:(h + 1) * DN_DK]
        q_ref[:, h * DN_DK:(h + 1) * DN_DK] = z * lax.rsqrt(jnp.sum(z * z, axis=-1, keepdims=True) + EPS) * (DN_DK ** -0.5)
        z = s[:, nqk + h * DN_DK:nqk + (h + 1) * DN_DK]
        k_ref[:, h * DN_DK:(h + 1) * DN_DK] = z * lax.rsqrt(jnp.sum(z * z, axis=-1, keepdims=True) + EPS)
    v_ref[...] = s[:, 2 * nqk:]
    z = ys_ref[...]
    lane = lax.broadcasted_iota(jnp.int32, z.shape, 1)
    g = -jnp.exp(av_ref[...]) * jax.nn.softplus(z + dtb_ref[...])
    beta = jax.nn.sigmoid(z)
    g_ref[...] = jnp.where(lane < SM_DB, g, jnp.where(lane < SM_GG, beta, 0.0))
    zh, zl = _split2(z)
    for d in range(2):
        la = _dot(zh, w2h_ref[d]) + _dot(zl, w2h_ref[d]) + _dot(zh, w2l_ref[d]) + b2_ref[d]
        la_ref[d] = jax.nn.log_sigmoid(la) / GLA_TAU


def _recprep(ym, ys, cw, av, dtb, w2h, w2l, b2, lay):
    R = ym.shape[0]
    tm = lay.tm
    halo = 8
    nh = tm // halo
    nhb = R // halo
    nd = DN_H * DN_DK
    body = functools.partial(_recprep_body, lay=lay, halo=halo)
    return pl.pallas_call(
        body,
        out_shape=(jax.ShapeDtypeStruct((R, nd), f32), jax.ShapeDtypeStruct((R, nd), f32),
                   jax.ShapeDtypeStruct((R, DN_H * DN_DV), f32), jax.ShapeDtypeStruct((R, LANES), f32),
                   jax.ShapeDtypeStruct((2, R, GLA_H * GLA_DK), f32)),
        grid=(R // tm,),
        in_specs=[pl.BlockSpec((halo, REC_QKV), lambda i: (jnp.maximum(i * nh - 1, 0), 0)),
                  pl.BlockSpec((tm, REC_QKV), lambda i: (i, 0)),
                  pl.BlockSpec((halo, REC_QKV), lambda i: (jnp.minimum((i + 1) * nh, nhb - 1), 0)),
                  pl.BlockSpec((tm, LANES), lambda i: (i, 0)),
                  _const_spec(cw.shape), _const_spec(av.shape), _const_spec(dtb.shape),
                  _const_spec(w2h.shape), _const_spec(w2l.shape), _const_spec(b2.shape)],
        out_specs=(pl.BlockSpec((tm, nd), lambda i: (i, 0)), pl.BlockSpec((tm, nd), lambda i: (i, 0)),
                   pl.BlockSpec((tm, DN_H * DN_DV), lambda i: (i, 0)), pl.BlockSpec((tm, LANES), lambda i: (i, 0)),
                   pl.BlockSpec((2, tm, GLA_H * GLA_DK), lambda i: (0, i, 0))),
        scratch_shapes=[pltpu.VMEM((tm + 2 * halo, REC_QKV), f32)],
        compiler_params=_cparams("parallel"),
        name="rec_prep",
    )(ym, ym, ym, ys, cw, av, dtb, w2h, w2l, b2)


def _tri_masks(rev, width):
    ri = lax.broadcasted_iota(jnp.int32, (CHUNK, width), 0)
    ci = lax.broadcasted_iota(jnp.int32, (CHUNK, width), 1) % CHUNK
    if rev:
        return ri <= ci, ri < ci, ri >= ci
    return ri >= ci, ri > ci, ri <= ci


def _block_diag(x, nblk):
    w = x.shape[1]
    t = jnp.concatenate([x] * nblk, axis=0)
    rb = lax.broadcasted_iota(jnp.int32, t.shape, 0) // CHUNK
    cb = lax.broadcasted_iota(jnp.int32, t.shape, 1) // (w // nblk)
    return jnp.where(rb == cb, t, jnp.zeros_like(t))


def _gate_cols(gt, d, width):
    H = DN_H
    g = [jnp.broadcast_to(gt[:, SM_DA + d * H + h:SM_DA + d * H + h + 1], (CHUNK, width)) for h in range(H)]
    b = [jnp.broadcast_to(gt[:, SM_DB + d * H + h:SM_DB + d * H + h + 1], (CHUNK, width)) for h in range(H)]
    return jnp.concatenate(g, axis=1), jnp.concatenate(b, axis=1)


def _delta_prep_body(q_ref, k_ref, v_ref, g_ref, w_ref, u_ref, qd_ref, kd_ref, in_ref, el_ref, *, nc):
    H = DN_H
    insts = [(c, d) for c in range(nc) for d in range(2)]
    rows = [slice(c * CHUNK, (c + 1) * CHUNK) for c in range(nc)]
    masks = [_tri_masks(d == 1, H * CHUNK) for d in range(2)]
    mbs = [_tri_masks(d == 1, CHUNK)[0].astype(bf16) for d in range(2)]
    neg1 = jnp.full((CHUNK, CHUNK), -1.0, bf16)
    ri = lax.broadcasted_iota(jnp.int32, (CHUNK, H * CHUNK), 0)
    ci = lax.broadcasted_iota(jnp.int32, (CHUNK, H * CHUNK), 1) % CHUNK

    kkqk = []
    for c in range(nc):
        k_bf = k_ref[rows[c], :].astype(bf16)
        kq = jnp.concatenate([k_bf, q_ref[rows[c], :].astype(bf16)], axis=0)
        kkqk.append(_dot_nt(kq, _block_diag(k_bf, H)))

    G, diff = {}, {}
    for c, d in insts:
        gt = g_ref[rows[c], :]
        incl_t4 = masks[d][2]
        g128, _ = _gate_cols(gt, d, DN_DK)
        g64, _ = _gate_cols(gt, d, CHUNK)
        hi, lo = _split2(g128)
        G[c, d] = _dot(jnp.concatenate([mbs[d], mbs[d]], axis=1), jnp.concatenate([hi, lo], axis=0))
        hi6, lo6 = _split2(g64)
        zero = jnp.zeros_like(hi6)
        rhs = jnp.concatenate([hi6, lo6, jnp.where(incl_t4, hi6, zero), jnp.where(incl_t4, lo6, zero)], axis=0)
        diff[c, d] = _dot(jnp.concatenate([mbs[d], mbs[d], neg1, neg1], axis=1), rhs)

    a, p = {}, {}
    for c, d in insts:
        incl4, strict4, _ = masks[d]
        decay = jnp.where(incl4, jnp.exp(diff[c, d]), 0.0)
        _, b64 = _gate_cols(g_ref[rows[c], :], d, CHUNK)
        a[c, d] = jnp.where(strict4, b64 * kkqk[c][:CHUNK] * decay, 0.0)
        in_ref[d, rows[c], :] = (kkqk[c][CHUNK:] * decay).astype(bf16)
        p[c, d] = jnp.where(ri == ci, 1.0, 0.0) - jnp.where(ri // 2 == ci // 2, a[c, d], 0.0)

    m = 2
    while m < CHUNK:
        sib = jnp.logical_and(ri // (2 * m) == ci // (2 * m), ri // m != ci // m)
        x = {}
        for i in insts:
            x[i] = _dot(p[i].astype(bf16), _block_diag(jnp.where(sib, a[i], 0.0).astype(bf16), H))
        for i in insts:
            p[i] = p[i] - _dot(x[i].astype(bf16), _block_diag(p[i].astype(bf16), H))
        m *= 2

    for c, d in insts:
        rs = rows[c]
        Gc = G[c, d]
        tb = p[c, d].astype(bf16)
        last = 0 if d else CHUNK - 1
        g_last = Gc[last:last + 1, :]
        e_g = jnp.exp(Gc)
        _, b128 = _gate_cols(g_ref[rs, :], d, DN_DK)
        k_all = k_ref[rs, :]
        vbeta = v_ref[rs, :] * b128
        kbg = k_all * b128 * e_g
        qd_ref[d, rs, :] = (q_ref[rs, :] * e_g).astype(bf16)
        kd_ref[d, rs, :] = (k_all * jnp.exp(g_last - Gc)).astype(bf16)
        el_ref[d, c] = jnp.exp(g_last)
        for h in range(H):
            sl = slice(h * DN_DK, (h + 1) * DN_DK)
            sc = slice(h * CHUNK, (h + 1) * CHUNK)
            uw = _dot(tb[:, sc], jnp.concatenate([vbeta[:, sl], kbg[:, sl]], axis=1).astype(bf16))
            u_ref[d, rs, sl] = uw[:, :DN_DV]
            w_ref[d, rs, sl] = uw[:, DN_DV:].astype(bf16)


def _delta_prep(q, k, v, gates, lay):
    R = q.shape[0]
    tm = lay.tm
    nc = tm // CHUNK
    wq = DN_H * DN_DK
    wv = DN_H * DN_DV
    wi = DN_H * CHUNK
    row = lambda i: (i, 0)
    drow = lambda i: (0, i, 0)
    body = functools.partial(_delta_prep_body, nc=nc)
    return pl.pallas_call(
        body,
        out_shape=(jax.ShapeDtypeStruct((2, R, wq), bf16), jax.ShapeDtypeStruct((2, R, wv), f32),
                   jax.ShapeDtypeStruct((2, R, wq), bf16), jax.ShapeDtypeStruct((2, R, wq), bf16),
                   jax.ShapeDtypeStruct((2, R, wi), bf16), jax.ShapeDtypeStruct((2, R // CHUNK, 1, wq), f32)),
        grid=(R // tm,),
        in_specs=[pl.BlockSpec((tm, wq), row), pl.BlockSpec((tm, wq), row), pl.BlockSpec((tm, wv), row),
                  pl.BlockSpec((tm, LANES), row)],
        out_specs=(pl.BlockSpec((2, tm, wq), drow), pl.BlockSpec((2, tm, wv), drow),
                   pl.BlockSpec((2, tm, wq), drow), pl.BlockSpec((2, tm, wq), drow),
                   pl.BlockSpec((2, tm, wi), drow), pl.BlockSpec((2, nc, 1, wq), lambda i: (0, i, 0, 0))),
        compiler_params=_cparams("parallel"),
        name="delta_prep",
    )(q, k, v, gates)


def _delta_scan_body(wf, uf, qdf, kdf, inf, elf, wb, ub, qdb, kdb, inb, elb, of_ref, ob_ref, s_sc, *, g):
    @pl.when(pl.program_id(1) == 0)
    def _():
        s_sc[...] = jnp.zeros_like(s_sc)

    views = ((wf, uf, qdf, kdf, inf, elf, of_ref), (wb, ub, qdb, kdb, inb, elb, ob_ref))
    chains = [(d, h) for d in range(2) for h in range(DN_H)]
    S = {ch: s_sc[ch[0], ch[1]] for ch in chains}
    for j in range(g):
        r1, vnb = {}, {}
        for d, h in chains:
            w_r, u_r, qd_r = views[d][0], views[d][1], views[d][2]
            rs = slice((g - 1 - j if d else j) * CHUNK, (g - j if d else j + 1) * CHUNK)
            sl = slice(h * DN_DK, (h + 1) * DN_DK)
            r1[d, h] = _dot(jnp.concatenate([w_r[0, rs, sl], qd_r[0, rs, sl]], axis=0), S[d, h].astype(bf16))
        for d, h in chains:
            u_r = views[d][1]
            rs = slice((g - 1 - j if d else j) * CHUNK, (g - j if d else j + 1) * CHUNK)
            sl = slice(h * DN_DK, (h + 1) * DN_DK)
            vnb[d, h] = (u_r[0, rs, sl] - r1[d, h][:CHUNK]).astype(bf16)
        for d, h in chains:
            kd_r, in_r, el_r, o_r = views[d][3], views[d][4], views[d][5], views[d][6]
            c = g - 1 - j if d else j
            rs = slice(c * CHUNK, (c + 1) * CHUNK)
            sl = slice(h * DN_DK, (h + 1) * DN_DK)
            sc = slice(h * CHUNK, (h + 1) * CHUNK)
            o_r[rs, sl] = r1[d, h][CHUNK:] + _dot(in_r[0, rs, sc], vnb[d, h])
            S[d, h] = S[d, h] * el_r[0, c][:, sl] + _dot_tn(kd_r[0, rs, sl], vnb[d, h])
    for d, h in chains:
        s_sc[d, h] = S[d, h]


def _group_maps(lay, g):
    B = lay.B
    ngc, ngl = lay.Tc // (CHUNK * g), lay.T // (CHUNK * g)

    def fwd(b, n):
        return jnp.where(n < ngc, b * ngc + n, B * ngc + b * ngl + (n - ngc))

    def bwd(b, n):
        return jnp.where(n < ngc, b * ngc + (ngc - 1 - n), B * ngc + b * ngl + (ngl - 1 - (n - ngc)))

    return fwd, bwd, ngc + ngl


def _scan_group(lay):
    g = min(4, lay.Tc // CHUNK)
    assert (lay.Tc // CHUNK) % g == 0 and (lay.T // CHUNK) % g == 0
    return g


def _delta_scan(w, u, qd, kd, intra, el, lay):
    R = w.shape[1]
    g = _scan_group(lay)
    fwd, bwd, nstep = _group_maps(lay, g)
    wq = DN_H * DN_DK
    wv = DN_H * DN_DV
    wi = DN_H * CHUNK
    rows = g * CHUNK

    def view(m, d):
        sp = lambda width: pl.BlockSpec((1, rows, width), lambda b, n: (d, m(b, n), 0))
        return [sp(wq), sp(wv), sp(wq), sp(wq), sp(wi), pl.BlockSpec((1, g, 1, wq), lambda b, n: (d, m(b, n), 0, 0))]

    def out(m):
        return pl.BlockSpec((rows, wv), lambda b, n: (m(b, n), 0))

    body = functools.partial(_delta_scan_body, g=g)
    return pl.pallas_call(
        body,
        out_shape=(jax.ShapeDtypeStruct((R, wv), f32), jax.ShapeDtypeStruct((R, wv), f32)),
        grid=(lay.B, nstep),
        in_specs=view(fwd, 0) + view(bwd, 1),
        out_specs=(out(fwd), out(bwd)),
        scratch_shapes=[pltpu.VMEM((2, DN_H, DN_DK, DN_DV), f32)],
        compiler_params=_cparams("parallel", "arbitrary"),
        name="delta_scan",
    )(w, u, qd, kd, intra, el, w, u, qd, kd, intra, el)


def _gla_prep_body(q_ref, k_ref, v_ref, la_ref, oi_ref, qi_ref, kv_ref, al_ref, *, nc):
    H = GLA_H
    insts = [(c, d) for c in range(nc) for d in range(2)]
    rows = [slice(c * CHUNK, (c + 1) * CHUNK) for c in range(nc)]
    incl4 = [_tri_masks(d == 1, H * CHUNK)[0] for d in range(2)]
    mbs = [_tri_masks(d == 1, CHUNK)[0].astype(bf16) for d in range(2)]
    b = {}
    for c, d in insts:
        hi, lo = _split2(la_ref[d, rows[c], :])
        b[c, d] = _dot(jnp.concatenate([mbs[d], mbs[d]], axis=1), jnp.concatenate([hi, lo], axis=0))
    att = {}
    for c, d in insts:
        q = q_ref[rows[c], :] * (GLA_DK ** -0.5)
        mid = CHUNK - 1 - CHUNK // 2 if d else CHUNK // 2
        b_mid = b[c, d][mid:mid + 1, :]
        qe = (q * jnp.exp(b[c, d] - b_mid)).astype(bf16)
        ke = (k_ref[rows[c], :] * jnp.exp(b_mid - b[c, d])).astype(bf16)
        att[c, d] = jnp.where(incl4[d], _dot_nt(qe, _block_diag(ke, H)), 0.0).astype(bf16)
        qi_ref[d, rows[c], :] = (q * jnp.exp(b[c, d])).astype(bf16)
    for c, d in insts:
        v_bf = v_ref[rows[c], :].astype(bf16)
        oi_ref[d, rows[c], :] = _dot(att[c, d], _block_diag(v_bf, H))
        last = 0 if d else CHUNK - 1
        b_last = b[c, d][last:last + 1, :]
        ks = (k_ref[rows[c], :] * jnp.exp(b_last - b[c, d])).astype(bf16)
        kv = [_dot_tn(v_bf[:, h * GLA_DV:(h + 1) * GLA_DV], ks[:, h * GLA_DK:(h + 1) * GLA_DK]) for h in range(H)]
        kv_ref[d, c] = jnp.concatenate(kv, axis=1)
        al_ref[d, c] = jnp.exp(b_last)


def _gla_prep(ym, la, lay):
    R = ym.shape[0]
    tm = lay.tm
    nc = tm // CHUNK
    wk = GLA_H * GLA_DK
    wv = GLA_H * GLA_DV
    cq, ck, cv = OFF_GQ // wk, OFF_GK // wk, OFF_GV // wv
    drow = lambda i: (0, i, 0)
    body = functools.partial(_gla_prep_body, nc=nc)
    return pl.pallas_call(
        body,
        out_shape=(jax.ShapeDtypeStruct((2, R, wv), f32), jax.ShapeDtypeStruct((2, R, wk), bf16),
                   jax.ShapeDtypeStruct((2, R // CHUNK, GLA_DV, wk), f32),
                   jax.ShapeDtypeStruct((2, R // CHUNK, 1, wk), f32)),
        grid=(R // tm,),
        in_specs=[pl.BlockSpec((tm, wk), lambda i: (i, cq)), pl.BlockSpec((tm, wk), lambda i: (i, ck)),
                  pl.BlockSpec((tm, wv), lambda i: (i, cv)), pl.BlockSpec((2, tm, wk), drow)],
        out_specs=(pl.BlockSpec((2, tm, wv), drow), pl.BlockSpec((2, tm, wk), drow),
                   pl.BlockSpec((2, nc, GLA_DV, wk), lambda i: (0, i, 0, 0)),
                   pl.BlockSpec((2, nc, 1, wk), lambda i: (0, i, 0, 0))),
        compiler_params=_cparams("parallel"),
        name="gla_prep",
    )(ym, ym, ym, la)


def _gla_scan_body(qif, kvf, alf, qib, kvb, alb, of_ref, ob_ref, st_sc, *, g):
    @pl.when(pl.program_id(1) == 0)
    def _():
        st_sc[...] = jnp.zeros_like(st_sc)

    views = ((qif, kvf, alf, of_ref), (qib, kvb, alb, ob_ref))
    for j in range(g):
        for d, (qi_r, kv_r, al_r, o_r) in enumerate(views):
            c = g - 1 - j if d else j
            rs = slice(c * CHUNK, (c + 1) * CHUNK)
            st = st_sc[d]
            st_bf = st.astype(bf16)
            for h in range(GLA_H):
                sk = slice(h * GLA_DK, (h + 1) * GLA_DK)
                o_r[rs, h * GLA_DV:(h + 1) * GLA_DV] = _dot_nt(qi_r[0, rs, sk], st_bf[:, sk])
            st_sc[d] = st * al_r[0, c] + kv_r[0, c]


def _gla_scan(qi, kv, al, lay):
    R = qi.shape[1]
    g = _scan_group(lay)
    fwd, bwd, nstep = _group_maps(lay, g)
    wk = GLA_H * GLA_DK
    wv = GLA_H * GLA_DV
    rows = g * CHUNK

    def view(m, d):
        return [pl.BlockSpec((1, rows, wk), lambda b, n: (d, m(b, n), 0)),
                pl.BlockSpec((1, g, GLA_DV, wk), lambda b, n: (d, m(b, n), 0, 0)),
                pl.BlockSpec((1, g, 1, wk), lambda b, n: (d, m(b, n), 0, 0))]

    def out(m):
        return pl.BlockSpec((rows, wv), lambda b, n: (m(b, n), 0))

    body = functools.partial(_gla_scan_body, g=g)
    return pl.pallas_call(
        body,
        out_shape=(jax.ShapeDtypeStruct((R, wv), f32), jax.ShapeDtypeStruct((R, wv), f32)),
        grid=(lay.B, nstep),
        in_specs=view(fwd, 0) + view(bwd, 1),
        out_specs=(out(fwd), out(bwd)),
        scratch_shapes=[pltpu.VMEM((2, GLA_DV, wk), f32)],
        compiler_params=_cparams("parallel", "arbitrary"),
        name="gla_scan",
    )(qi, kv, al, qi, kv, al)


def _recout_body(df_ref, db_ref, gf_ref, gb_ref, gi_ref, z_ref, r_ref, x_ref, m_ref, dn_ref, gn_ref, w_ref, o_ref):
    dn = df_ref[...] + db_ref[...]
    gl = (gf_ref[...] + gi_ref[0]) + (gb_ref[...] + gi_ref[1])
    z = z_ref[...]
    r = r_ref[...]
    parts = []
    for src, gate, gain, hd, nh in ((dn, z, dn_ref[...], DN_DV, DN_H), (gl, r, gn_ref[...], GLA_DV, GLA_H)):
        for h in range(nh):
            a = src[:, h * hd:(h + 1) * hd]
            a = a * lax.rsqrt(jnp.mean(a * a, axis=-1, keepdims=True) + EPS) * gain
            parts.append(a * _silu(gate[:, h * hd:(h + 1) * hd]))
    mix = jnp.concatenate(parts, axis=1).astype(bf16)
    o_ref[...] = x_ref[...] + m_ref[0][2:3, :] * _dot(mix, w_ref[...])


def _recout(dn_f, dn_b, gl_f, gl_b, gl_i, ym, xs, mods_l, dn_norm, gla_norm, w, lay):
    R, D = xs.shape
    tm = lay.tmf
    wd = DN_H * DN_DV
    wg = GLA_H * GLA_DV
    row = lambda i: (i, 0)
    return pl.pallas_call(
        _recout_body,
        out_shape=jax.ShapeDtypeStruct((R, D), f32),
        grid=(R // tm,),
        in_specs=[pl.BlockSpec((tm, wd), row), pl.BlockSpec((tm, wd), row),
                  pl.BlockSpec((tm, wg), row), pl.BlockSpec((tm, wg), row),
                  pl.BlockSpec((2, tm, wg), lambda i: (0, i, 0)),
                  pl.BlockSpec((tm, wd), lambda i: (i, OFF_DZ // wd)),
                  pl.BlockSpec((tm, wg), lambda i: (i, OFF_GR // wg)),
                  pl.BlockSpec((tm, D), row),
                  pl.BlockSpec((1, 6, D), lambda i: (lay.mod_row(i * tm), 0, 0)),
                  _const_spec(dn_norm.shape), _const_spec(gla_norm.shape), _const_spec(w.shape)],
        out_specs=pl.BlockSpec((tm, D), row),
        compiler_params=_cparams("parallel"),
        name="rec_outproj",
    )(dn_f, dn_b, gl_f, gl_b, gl_i, ym, ym, xs, mods_l, dn_norm, gla_norm, w)


def _final_body(x_ref, g_ref, o_ref):
    x = x_ref[...]
    o_ref[...] = x * lax.rsqrt(jnp.mean(x * x, axis=-1, keepdims=True) + EPS) * g_ref[...]


def _final(xs, gain, lay):
    R, D = xs.shape
    tm = lay.tm
    off = lay.B * lay.Tc // tm
    n = lay.B * lay.T
    return pl.pallas_call(
        _final_body,
        out_shape=jax.ShapeDtypeStruct((n, D), f32),
        grid=(n // tm,),
        in_specs=[pl.BlockSpec((tm, D), lambda i: (i + off, 0)), _const_spec(gain.shape)],
        out_specs=pl.BlockSpec((tm, D), lambda i: (i, 0)),
        compiler_params=_cparams("parallel"),
        name="final_norm",
    )(xs, gain)


def _rec_weights(w_in):
    sizes = (DN_H * DN_DK, DN_H * DN_DK, DN_H * DN_DV, DN_H * DN_DV, 2 * DN_H, 2 * DN_H,
             GLA_H * GLA_DK, GLA_H * GLA_DK, GLA_H * GLA_DV, GLA_H * GLA_DV, 2 * GLA_RANK)
    offs = [0]
    for s in sizes:
        offs.append(offs[-1] + s)
    seg = [w_in[:, offs[i]:offs[i + 1]] for i in range(len(sizes))]
    dq, dk, dv, dz, da, db, gq, gk, gv, gr, gg = seg
    main = jnp.concatenate([dq, dk, dv, dz, gq, gk, gv, gr], axis=1).astype(bf16)
    small = jnp.concatenate([da, db, gg], axis=1)
    small = jnp.pad(small, ((0, 0), (0, LANES - small.shape[1]))).astype(bf16)
    return main, small


def _pad_lanes(v, off=0):
    v = v.reshape(1, -1)
    return jnp.pad(v, ((0, 0), (off, LANES - off - v.shape[1])))


def _gla_gate_weights(w2):
    out = jnp.zeros((2, LANES, w2.shape[2]), f32)
    for d in range(2):
        out = out.at[d, SM_GG + d * GLA_RANK:SM_GG + (d + 1) * GLA_RANK, :].set(w2[d])
    hi = out.astype(bf16)
    lo = (out - hi.astype(f32)).astype(bf16)
    return hi, lo


def _ffn_weights(w_up, conv, w_down):
    D, F2 = w_up.shape
    F = F2 // 2
    n = F // FFN_CHUNK
    wg = w_up[:, :F].astype(bf16).reshape(D, n, FFN_CHUNK).transpose(1, 0, 2)
    wv = w_up[:, F:].astype(bf16).reshape(D, n, FFN_CHUNK).transpose(1, 0, 2)
    cw = conv.reshape(3, n, FFN_CHUNK).transpose(1, 0, 2)
    wd = w_down.astype(bf16).reshape(n, FFN_CHUNK, D)
    return wg, wv, cw, wd


def _rope_tables(lay):
    T, Tc, B = lay.T, lay.Tc, lay.B
    t = jnp.arange(T, dtype=jnp.int32)
    row = (t // GRID_W).astype(f32)
    col = (t % GRID_W).astype(f32)
    inv_freq = ROPE_THETA ** (-jnp.arange(ROPE_PAIRS, dtype=f32) / ROPE_PAIRS)
    ar = row[:, None] * inv_freq
    ac = col[:, None] * inv_freq
    cos = jnp.concatenate([jnp.cos(ar), jnp.cos(ar), jnp.cos(ac), jnp.cos(ac)], axis=1)
    sin = jnp.concatenate([-jnp.sin(ar), jnp.sin(ar), -jnp.sin(ac), jnp.sin(ac)], axis=1)
    cos = jnp.concatenate([jnp.ones((B * Tc, ATT_HD), f32)] + [cos] * B, axis=0)
    sin = jnp.concatenate([jnp.zeros((B * Tc, ATT_HD), f32)] + [sin] * B, axis=0)
    return cos, sin


def kernel(x, c, ctx, c_ctx, mod_w, mod_b, rec_w_in, rec_conv, dn_a_log, dn_dt_bias, dn_norm, gla_w2, gla_b2,
           gla_norm, rec_w_out, att_w_qkv, att_q_norm, att_k_norm, att_w_out, ffn_w_up, ffn_conv, ffn_w_down,
           final_norm):
    B, T, D = x.shape
    Tc = ctx.shape[1]
    depth = mod_w.shape[0]
    lay = _Layout(B, T, Tc)
    assert B + 1 <= MOD_ROWS
    xs = jnp.concatenate([ctx.reshape(B * Tc, D), x.reshape(B * T, D)], axis=0)
    cond = jnp.concatenate([c, c_ctx[None], jnp.zeros((MOD_ROWS - B - 1, D), f32)], axis=0)
    mods = _mods(cond, mod_w, mod_b)
    cos, sin = _rope_tables(lay)

    for i in range(depth):
        ml = mods[i]
        if i % 2 == 0:
            e = i // 2
            wm, ws = _rec_weights(rec_w_in[e])
            ym, ys = _recin(xs, ml, wm, ws, lay)
            w2h, w2l = _gla_gate_weights(gla_w2[e])
            q, k, v, gates, la = _recprep(
                ym, ys, rec_conv[e], _pad_lanes(dn_a_log[e], SM_DA), _pad_lanes(dn_dt_bias[e], SM_DA),
                w2h, w2l, gla_b2[e].reshape(2, 1, -1), lay)
            dn_f, dn_b = _delta_scan(*_delta_prep(q, k, v, gates, lay), lay)
            gl_i, qi, kv, al = _gla_prep(ym, la, lay)
            gl_f, gl_b = _gla_scan(qi, kv, al, lay)
            xs = _recout(dn_f, dn_b, gl_f, gl_b, gl_i, ym, xs, ml, dn_norm[e].reshape(1, -1),
                         gla_norm[e].reshape(1, -1), rec_w_out[e].astype(bf16), lay)
        else:
            o = i // 2
            q, k, v = _qkv(xs, ml, att_w_qkv[o].astype(bf16), att_q_norm[o].reshape(1, -1),
                           att_k_norm[o].reshape(1, -1), cos, sin, lay)
            a = _flash(q, k, v, lay)
            xs = _outproj(a, xs, ml, att_w_out[o].astype(bf16), lay)
        wg, wv, cw, wd = _ffn_weights(ffn_w_up[i], ffn_conv[i], ffn_w_down[i])
        xs = _ffn(xs, ml, wg, wv, cw, wd, lay)

    out = _final(xs, final_norm.reshape(1, -1), lay)
    return out.reshape(B, T, D)
```

```python
import functools
import math

import jax
import jax.numpy as jnp
from jax import lax
from jax.experimental import pallas as pl
from jax.experimental.pallas import tpu as pltpu

f32 = jnp.float32
bf16 = jnp.bfloat16

EPS = 1e-6
GRID_W = 64
DN_H = 4
DN_DK = 128
DN_DV = 128
GLA_H = 4
GLA_DK = 64
GLA_DV = 128
GLA_RANK = 16
GLA_TAU = 16.0
CHUNK = 64
ATT_H = 8
ATT_KVH = 2
ATT_G = ATT_H // ATT_KVH
ATT_HD = 128
ROPE_THETA = 10000.0
ROPE_PAIRS = ATT_HD // 4
LOG2E = math.log2(math.e)
FFN_CHUNK = 256
MOD_ROWS = 8
LANES = 128
VMEM_LIMIT = 56 * 1024 * 1024


def _cparams(*sem):
    return pltpu.CompilerParams(dimension_semantics=sem, vmem_limit_bytes=VMEM_LIMIT)


def _const_spec(shape):
    nd = len(shape)
    return pl.BlockSpec(shape, lambda *_: (0,) * nd, pipeline_mode=pl.Buffered(1))


class _Layout:
    def __init__(self, B, T, Tc):
        self.B, self.T, self.Tc = B, T, Tc
        self.R = B * Tc + B * T
        self.tm = min(256, Tc)
        assert Tc % self.tm == 0 and T % self.tm == 0 and T % CHUNK == 0 and Tc % CHUNK == 0
        self.tmf = min(512, B * Tc)
        assert (B * Tc) % self.tmf == 0 and T % self.tmf == 0
        self.seq_starts = [b * Tc for b in range(B)] + [B * Tc + b * T for b in range(B)]

    def mod_row(self, row0):
        return jnp.where(row0 < self.B * self.Tc, self.B, (row0 - self.B * self.Tc) // self.T)

    def is_seq_start(self, row0):
        return jnp.where(row0 < self.B * self.Tc, row0 % self.Tc == 0, (row0 - self.B * self.Tc) % self.T == 0)

    def edge_masks(self, rows):
        first = rows == self.seq_starts[0]
        last = rows == self.R - 1
        for s in self.seq_starts[1:]:
            first = jnp.logical_or(first, rows == s)
            last = jnp.logical_or(last, rows == s - 1)
        return first, last


def _normmod(x, m, shift_i, scale_i):
    ms = jnp.mean(x * x, axis=-1, keepdims=True)
    return (x * lax.rsqrt(ms + EPS)) * (1.0 + m[scale_i:scale_i + 1, :]) + m[shift_i:shift_i + 1, :]


def _silu(x):
    return x * jax.nn.sigmoid(x)


def _split2(x):
    hi = x.astype(bf16)
    lo = (x - hi.astype(f32)).astype(bf16)
    return hi, lo


def _dot(a, b):
    return jnp.dot(a, b, preferred_element_type=f32)


def _dot_nt(a, b):
    return lax.dot_general(a, b, (((1,), (1,)), ((), ())), preferred_element_type=f32)


def _dot_tn(a, b):
    return lax.dot_general(a, b, (((0,), (0,)), ((), ())), preferred_element_type=f32)


def _mod_body(c_ref, w_ref, b_ref, o_ref):
    a = _silu(c_ref[...]).astype(bf16)
    o_ref[0] = _dot(a, w_ref[0].astype(bf16)) + b_ref[0]


def _mods(cond, mod_w, mod_b):
    depth, D, D6 = mod_w.shape
    tn = 1536
    out = pl.pallas_call(
        _mod_body,
        out_shape=jax.ShapeDtypeStruct((depth, MOD_ROWS, D6), f32),
        grid=(depth, D6 // tn),
        in_specs=[pl.BlockSpec((MOD_ROWS, D), lambda l, j: (0, 0)),
                  pl.BlockSpec((1, D, tn), lambda l, j: (l, 0, j)),
                  pl.BlockSpec((1, 1, tn), lambda l, j: (l, 0, j))],
        out_specs=pl.BlockSpec((1, MOD_ROWS, tn), lambda l, j: (l, 0, j)),
        compiler_params=_cparams("parallel", "parallel"),
        name="mods",
    )(cond, mod_w, mod_b.reshape(depth, 1, D6))
    return out.reshape(depth, MOD_ROWS, 6, D)


def _ffn_body(xp_ref, x_ref, xn_ref, m_ref, wg_ref, wv_ref, cw_ref, wd_ref, *rest, lay, halo, ncf, off, final):
    fg_ref, o_ref, h_sc, g_sc, acc_sc = rest if final else (None,) + rest
    tm = lay.tmf
    row0 = (pl.program_id(0) + off) * tm
    m = m_ref[0]
    x = x_ref[...]
    xe = jnp.concatenate([xp_ref[...], x, xn_ref[...]], axis=0)
    h_sc[...] = _normmod(xe, m, 3, 4).astype(bf16)
    kill_prev, kill_next = lay.edge_masks(row0 + lax.broadcasted_iota(jnp.int32, (tm, 1), 0))
    acc_sc[...] = jnp.zeros_like(acc_sc)

    def step(c, carry):
        g_sc[...] = _dot(h_sc[...], wg_ref[c])
        val = _dot(h_sc[pl.ds(halo, tm), :], wv_ref[c])
        cw = cw_ref[c]
        gm1 = jnp.where(kill_prev, 0.0, g_sc[pl.ds(halo - 1, tm), :])
        g0 = g_sc[pl.ds(halo, tm), :]
        gp1 = jnp.where(kill_next, 0.0, g_sc[pl.ds(halo + 1, tm), :])
        gate = gm1 * cw[0:1, :] + g0 * cw[1:2, :] + gp1 * cw[2:3, :]
        act = (_silu(gate) * val).astype(bf16)
        acc_sc[...] += _dot(act, wd_ref[c])
        return carry

    lax.fori_loop(0, ncf, step, 0, unroll=True)
    y = x + m[5:6, :] * acc_sc[...]
    if final:
        y = y * lax.rsqrt(jnp.mean(y * y, axis=-1, keepdims=True) + EPS) * fg_ref[...]
    o_ref[...] = y


def _ffn(xs, mods_l, wg, wv, cw, wd, lay, final_gain=None):
    R, D = xs.shape
    tm = lay.tmf
    halo = 16
    ncf = wg.shape[0]
    cf = wg.shape[2]
    nh = tm // halo
    nhb = R // halo
    final = final_gain is not None
    off = lay.B * lay.Tc // tm if final else 0
    body = functools.partial(_ffn_body, lay=lay, halo=halo, ncf=ncf, off=off, final=final)
    extra_specs = [_const_spec(final_gain.shape)] if final else []
    extra_args = [final_gain] if final else []
    return pl.pallas_call(
        body,
        out_shape=jax.ShapeDtypeStruct((R - off * tm, D), f32),
        grid=(R // tm - off,),
        in_specs=[pl.BlockSpec((halo, D), lambda i: (jnp.maximum((i + off) * nh - 1, 0), 0)),
                  pl.BlockSpec((tm, D), lambda i: (i + off, 0)),
                  pl.BlockSpec((halo, D), lambda i: (jnp.minimum((i + off + 1) * nh, nhb - 1), 0)),
                  pl.BlockSpec((1, 6, D), lambda i: (lay.mod_row((i + off) * tm), 0, 0)),
                  _const_spec(wg.shape), _const_spec(wv.shape), _const_spec(cw.shape), _const_spec(wd.shape)]
        + extra_specs,
        out_specs=pl.BlockSpec((tm, D), lambda i: (i, 0)),
        scratch_shapes=[pltpu.VMEM((tm + 2 * halo, D), bf16),
                        pltpu.VMEM((tm + 2 * halo, cf), f32),
                        pltpu.VMEM((tm, D), f32)],
        compiler_params=_cparams("parallel"),
        name="conv_ffn",
    )(xs, xs, xs, mods_l, wg, wv, cw, wd, *extra_args)


def _qkv_body(x_ref, m_ref, w_ref, qn_ref, kn_ref, cos_ref, sin_ref, q_ref, k_ref, v_ref):
    h = _normmod(x_ref[...], m_ref[0], 0, 1).astype(bf16)
    y = _dot(h, w_ref[...])
    cos = cos_ref[...]
    sin = sin_ref[...]
    lane = lax.broadcasted_iota(jnp.int32, cos.shape, 1)
    first = (lane % (2 * ROPE_PAIRS)) < ROPE_PAIRS

    def norm_rope(z, gain):
        z = z * lax.rsqrt(jnp.mean(z * z, axis=-1, keepdims=True) + EPS) * gain
        partner = jnp.where(first, pltpu.roll(z, LANES - ROPE_PAIRS, axis=1), pltpu.roll(z, ROPE_PAIRS, axis=1))
        return z * cos + partner * sin

    nq = ATT_H * ATT_HD
    for hh in range(ATT_H):
        z = norm_rope(y[:, hh * ATT_HD:(hh + 1) * ATT_HD], qn_ref[...])
        q_ref[:, hh * ATT_HD:(hh + 1) * ATT_HD] = (z * (ATT_HD ** -0.5 * LOG2E)).astype(bf16)
    for kk in range(ATT_KVH):
        z = norm_rope(y[:, nq + kk * ATT_HD:nq + (kk + 1) * ATT_HD], kn_ref[...])
        k_ref[:, kk * ATT_HD:(kk + 1) * ATT_HD] = z.astype(bf16)
    nk = nq + ATT_KVH * ATT_HD
    ones = jnp.ones((y.shape[0], ATT_HD), bf16)
    for kk in range(ATT_KVH):
        v_ref[:, 2 * kk * ATT_HD:(2 * kk + 1) * ATT_HD] = y[:, nk + kk * ATT_HD:nk + (kk + 1) * ATT_HD].astype(bf16)
        v_ref[:, (2 * kk + 1) * ATT_HD:(2 * kk + 2) * ATT_HD] = ones


def _qkv(xs, mods_l, w, qn, kn, cos, sin, lay):
    R, D = xs.shape
    tm = lay.tmf
    nq = ATT_H * ATT_HD
    nkv = ATT_KVH * ATT_HD
    return pl.pallas_call(
        _qkv_body,
        out_shape=(jax.ShapeDtypeStruct((R, nq), bf16), jax.ShapeDtypeStruct((R, nkv), bf16),
                   jax.ShapeDtypeStruct((R, 2 * nkv), bf16)),
        grid=(R // tm,),
        in_specs=[pl.BlockSpec((tm, D), lambda i: (i, 0)),
                  pl.BlockSpec((1, 6, D), lambda i: (lay.mod_row(i * tm), 0, 0)),
                  _const_spec(w.shape), _const_spec(qn.shape), _const_spec(kn.shape),
                  pl.BlockSpec((tm, ATT_HD), lambda i: (i, 0)),
                  pl.BlockSpec((tm, ATT_HD), lambda i: (i, 0))],
        out_specs=(pl.BlockSpec((tm, nq), lambda i: (i, 0)), pl.BlockSpec((tm, nkv), lambda i: (i, 0)),
                   pl.BlockSpec((tm, 2 * nkv), lambda i: (i, 0))),
        compiler_params=_cparams("parallel"),
        name="qkv_rope",
    )(xs, mods_l, w, qn, kn, cos, sin)


def _flash_body(q_ref, kc_ref, vc_ref, kl_ref, vl_ref, o_ref, m_sc, acc_sc, *, tq, ncq, ck):
    qi = pl.program_id(2)
    q = q_ref[...]
    q4 = jnp.concatenate([q[:, g * ATT_HD:(g + 1) * ATT_HD] for g in range(ATT_G)], axis=0)
    m_sc[...] = jnp.full_like(m_sc, -jnp.inf)
    acc_sc[...] = jnp.zeros_like(acc_sc)

    def chunk(s, v):
        m_prev = m_sc[...]
        m_new = jnp.maximum(m_prev, jnp.max(s, axis=-1, keepdims=True))
        p = jnp.exp2(s - jnp.tile(m_new, (1, s.shape[1] // LANES)))
        alpha = jnp.exp2(m_prev - m_new)
        acc_sc[...] = jnp.tile(alpha, (1, 2)) * acc_sc[...] + _dot(p.astype(bf16), v)
        m_sc[...] = m_new

    chunk(_dot_nt(q4, kc_ref[...]), vc_ref[...])

    @pl.when(qi >= ncq)
    def _():
        n = kl_ref.shape[0] // ck
        s_next = _dot_nt(q4, kl_ref[0:ck, :])
        for c in range(n):
            s = s_next
            if c + 1 < n:
                s_next = _dot_nt(q4, kl_ref[(c + 1) * ck:(c + 2) * ck, :])
            chunk(s, vl_ref[c * ck:(c + 1) * ck, :])

    acc = acc_sc[...]
    out = acc[:, :ATT_HD] / acc[:, ATT_HD:]
    for g in range(ATT_G):
        o_ref[:, g * ATT_HD:(g + 1) * ATT_HD] = out[g * tq:(g + 1) * tq, :].astype(bf16)


def _flash(q, k, v1, lay):
    R = q.shape[0]
    B, T, Tc = lay.B, lay.T, lay.Tc
    tq = lay.tm
    ncq, nlq = Tc // tq, T // tq
    gw = ATT_G * ATT_HD
    ck = min(2048, T)
    assert T % ck == 0
    kc, kl = k[:B * Tc], k[B * Tc:]
    vc, vl = v1[:B * Tc], v1[B * Tc:]

    def q_map(b, kv, qi):
        return (jnp.where(qi < ncq, b * ncq + qi, B * ncq + b * nlq + (qi - ncq)), kv)

    kv_map = lambda b, kv, qi: (b, kv)
    body = functools.partial(_flash_body, tq=tq, ncq=ncq, ck=ck)
    return pl.pallas_call(
        body,
        out_shape=jax.ShapeDtypeStruct((R, ATT_H * ATT_HD), bf16),
        grid=(B, ATT_KVH, ncq + nlq),
        in_specs=[pl.BlockSpec((tq, gw), q_map),
                  pl.BlockSpec((Tc, ATT_HD), kv_map), pl.BlockSpec((Tc, 2 * ATT_HD), kv_map),
                  pl.BlockSpec((T, ATT_HD), kv_map), pl.BlockSpec((T, 2 * ATT_HD), kv_map)],
        out_specs=pl.BlockSpec((tq, gw), q_map),
        scratch_shapes=[pltpu.VMEM((ATT_G * tq, LANES), f32), pltpu.VMEM((ATT_G * tq, 2 * ATT_HD), f32)],
        compiler_params=_cparams("parallel", "parallel", "parallel"),
        name="flash_gqa",
    )(q, kc, vc, kl, vl)


def _outproj_body(a_ref, x_ref, m_ref, w_ref, o_ref):
    o_ref[...] = x_ref[...] + m_ref[0][2:3, :] * _dot(a_ref[...], w_ref[...])


def _outproj(a, xs, mods_l, w, lay):
    R, D = xs.shape
    tm = lay.tmf
    K = a.shape[1]
    return pl.pallas_call(
        _outproj_body,
        out_shape=jax.ShapeDtypeStruct((R, D), f32),
        grid=(R // tm,),
        in_specs=[pl.BlockSpec((tm, K), lambda i: (i, 0)),
                  pl.BlockSpec((tm, D), lambda i: (i, 0)),
                  pl.BlockSpec((1, 6, D), lambda i: (lay.mod_row(i * tm), 0, 0)),
                  _const_spec(w.shape)],
        out_specs=pl.BlockSpec((tm, D), lambda i: (i, 0)),
        compiler_params=_cparams("parallel"),
        name="att_outproj",
    )(a, xs, mods_l, w)


REC_MAIN = 2 * DN_H * DN_DK + 2 * DN_H * DN_DV + 2 * GLA_H * GLA_DK + 2 * GLA_H * GLA_DV
REC_QKV = 2 * DN_H * DN_DK + DN_H * DN_DV
OFF_DZ = REC_QKV
OFF_GQ = OFF_DZ + DN_H * DN_DV
OFF_GK = OFF_GQ + GLA_H * GLA_DK
OFF_GV = OFF_GK + GLA_H * GLA_DK
OFF_GR = OFF_GV + GLA_H * GLA_DV
SM_DA = 0
SM_DB = 2 * DN_H
SM_GG = 4 * DN_H


def _recin_body(x_ref, m_ref, wm_ref, ws_ref, ym_ref, ys_ref):
    h = _normmod(x_ref[...], m_ref[0], 0, 1).astype(bf16)
    ym_ref[...] = _dot(h, wm_ref[...])
    ys_ref[...] = _dot(h, ws_ref[...])


def _recin(xs, mods_l, wm, ws, lay):
    R, D = xs.shape
    tm = lay.tmf
    return pl.pallas_call(
        _recin_body,
        out_shape=(jax.ShapeDtypeStruct((R, REC_MAIN), f32), jax.ShapeDtypeStruct((R, LANES), f32)),
        grid=(R // tm,),
        in_specs=[pl.BlockSpec((tm, D), lambda i: (i, 0)),
                  pl.BlockSpec((1, 6, D), lambda i: (lay.mod_row(i * tm), 0, 0)),
                  _const_spec(wm.shape), _const_spec(ws.shape)],
        out_specs=(pl.BlockSpec((tm, REC_MAIN), lambda i: (i, 0)), pl.BlockSpec((tm, LANES), lambda i: (i, 0))),
        compiler_params=_cparams("parallel"),
        name="rec_inproj",
    )(xs, mods_l, wm, ws)


def _recprep_body(yp_ref, y_ref, yn_ref, ys_ref, cw_ref, av_ref, dtb_ref, w2h_ref, w2l_ref, b2_ref,
                  q_ref, k_ref, v_ref, g_ref, la_ref, e_sc, *, lay, halo):
    tm = lay.tm
    row0 = pl.program_id(0) * tm
    e_sc[0:halo, :] = yp_ref[...]
    e_sc[halo:halo + tm, :] = y_ref[...]
    e_sc[halo + tm:halo + tm + halo, :] = yn_ref[...]
    rowi = lax.broadcasted_iota(jnp.int32, (tm, 1), 0)
    kill_prev = jnp.logical_and(rowi == 0, lay.is_seq_start(row0))
    kill_next = jnp.logical_and(rowi == tm - 1, jnp.logical_or(lay.is_seq_start(row0 + tm), row0 + tm == lay.R))
    cw = cw_ref[...]
    gm1 = jnp.where(kill_prev, 0.0, e_sc[pl.ds(halo - 1, tm), :])
    g0 = e_sc[pl.ds(halo, tm), :]
    gp1 = jnp.where(kill_next, 0.0, e_sc[pl.ds(halo + 1, tm), :])
    s = _silu(gm1 * cw[0:1, :] + g0 * cw[1:2, :] + gp1 * cw[2:3, :])
    nqk = DN_H * DN_DK
    for h in range(DN_H):
        z = s[:, h * DN_DK:(h + 1) * DN_DK]
        q_ref[:, h * DN_DK:(h + 1) * DN_DK] = z * lax.rsqrt(jnp.sum(z * z, axis=-1, keepdims=True) + EPS) * (DN_DK ** -0.5)
        z = s[:, nqk + h * DN_DK:nqk + (h + 1) * DN_DK]
        k_ref[:, h * DN_DK:(h + 1) * DN_DK] = z * lax.rsqrt(jnp.sum(z * z, axis=-1, keepdims=True) + EPS)
    v_ref[...] = s[:, 2 * nqk:]
    z = ys_ref[...]
    lane = lax.broadcasted_iota(jnp.int32, z.shape, 1)
    g = -jnp.exp(av_ref[...]) * jax.nn.softplus(z + dtb_ref[...])
    beta = jax.nn.sigmoid(z)
    g_ref[...] = jnp.where(lane < SM_DB, g, jnp.where(lane < SM_GG, beta, 0.0))
    zh, zl = _split2(z)
    for d in range(2):
        la = _dot(zh, w2h_ref[d]) + _dot(zl, w2h_ref[d]) + _dot(zh, w2l_ref[d]) + b2_ref[d]
        la_ref[d] = jax.nn.log_sigmoid(la) / GLA_TAU


def _recprep(ym, ys, cw, av, dtb, w2h, w2l, b2, lay):
    R = ym.shape[0]
    tm = lay.tm
    halo = 8
    nh = tm // halo
    nhb = R // halo
    nd = DN_H * DN_DK
    body = functools.partial(_recprep_body, lay=lay, halo=halo)
    return pl.pallas_call(
        body,
        out_shape=(jax.ShapeDtypeStruct((R, nd), f32), jax.ShapeDtypeStruct((R, nd), f32),
                   jax.ShapeDtypeStruct((R, DN_H * DN_DV), f32), jax.ShapeDtypeStruct((R, LANES), f32),
                   jax.ShapeDtypeStruct((2, R, GLA_H * GLA_DK), f32)),
        grid=(R // tm,),
        in_specs=[pl.BlockSpec((halo, REC_QKV), lambda i: (jnp.maximum(i * nh - 1, 0), 0)),
                  pl.BlockSpec((tm, REC_QKV), lambda i: (i, 0)),
                  pl.BlockSpec((halo, REC_QKV), lambda i: (jnp.minimum((i + 1) * nh, nhb - 1), 0)),
                  pl.BlockSpec((tm, LANES), lambda i: (i, 0)),
                  _const_spec(cw.shape), _const_spec(av.shape), _const_spec(dtb.shape),
                  _const_spec(w2h.shape), _const_spec(w2l.shape), _const_spec(b2.shape)],
        out_specs=(pl.BlockSpec((tm, nd), lambda i: (i, 0)), pl.BlockSpec((tm, nd), lambda i: (i, 0)),
                   pl.BlockSpec((tm, DN_H * DN_DV), lambda i: (i, 0)), pl.BlockSpec((tm, LANES), lambda i: (i, 0)),
                   pl.BlockSpec((2, tm, GLA_H * GLA_DK), lambda i: (0, i, 0))),
        scratch_shapes=[pltpu.VMEM((tm + 2 * halo, REC_QKV), f32)],
        compiler_params=_cparams("parallel"),
        name="rec_prep",
    )(ym, ym, ym, ys, cw, av, dtb, w2h, w2l, b2)


def _tri_masks(rev, width):
    ri = lax.broadcasted_iota(jnp.int32, (CHUNK, width), 0)
    ci = lax.broadcasted_iota(jnp.int32, (CHUNK, width), 1) % CHUNK
    if rev:
        return ri <= ci, ri < ci, ri >= ci
    return ri >= ci, ri > ci, ri <= ci


def _block_diag(x, nblk):
    w = x.shape[1]
    t = jnp.concatenate([x] * nblk, axis=0)
    rb = lax.broadcasted_iota(jnp.int32, t.shape, 0) // CHUNK
    cb = lax.broadcasted_iota(jnp.int32, t.shape, 1) // (w // nblk)
    return jnp.where(rb == cb, t, jnp.zeros_like(t))


def _gate_cols(gt, d, width):
    H = DN_H
    g = [jnp.broadcast_to(gt[:, SM_DA + d * H + h:SM_DA + d * H + h + 1], (CHUNK, width)) for h in range(H)]
    b = [jnp.broadcast_to(gt[:, SM_DB + d * H + h:SM_DB + d * H + h + 1], (CHUNK, width)) for h in range(H)]
    return jnp.concatenate(g, axis=1), jnp.concatenate(b, axis=1)


def _delta_prep_body(q_ref, k_ref, v_ref, g_ref, w_ref, u_ref, qd_ref, kd_ref, in_ref, el_ref, *, nc):
    H = DN_H
    insts = [(c, d) for c in range(nc) for d in range(2)]
    rows = [slice(c * CHUNK, (c + 1) * CHUNK) for c in range(nc)]
    masks = [_tri_masks(d == 1, H * CHUNK) for d in range(2)]
    mbs = [_tri_masks(d == 1, CHUNK)[0].astype(bf16) for d in range(2)]
    neg1 = jnp.full((CHUNK, CHUNK), -1.0, bf16)
    ri = lax.broadcasted_iota(jnp.int32, (CHUNK, H * CHUNK), 0)
    ci = lax.broadcasted_iota(jnp.int32, (CHUNK, H * CHUNK), 1) % CHUNK

    kkqk = []
    for c in range(nc):
        k_bf = k_ref[rows[c], :].astype(bf16)
        kq = jnp.concatenate([k_bf, q_ref[rows[c], :].astype(bf16)], axis=0)
        kkqk.append(_dot_nt(kq, _block_diag(k_bf, H)))

    G, diff = {}, {}
    for c, d in insts:
        gt = g_ref[rows[c], :]
        incl_t4 = masks[d][2]
        g128, _ = _gate_cols(gt, d, DN_DK)
        g64, _ = _gate_cols(gt, d, CHUNK)
        hi, lo = _split2(g128)
        G[c, d] = _dot(jnp.concatenate([mbs[d], mbs[d]], axis=1), jnp.concatenate([hi, lo], axis=0))
        hi6, lo6 = _split2(g64)
        zero = jnp.zeros_like(hi6)
        rhs = jnp.concatenate([hi6, lo6, jnp.where(incl_t4, hi6, zero), jnp.where(incl_t4, lo6, zero)], axis=0)
        diff[c, d] = _dot(jnp.concatenate([mbs[d], mbs[d], neg1, neg1], axis=1), rhs)

    a, p = {}, {}
    for c, d in insts:
        incl4, strict4, _ = masks[d]
        decay = jnp.where(incl4, jnp.exp(diff[c, d]), 0.0)
        _, b64 = _gate_cols(g_ref[rows[c], :], d, CHUNK)
        a[c, d] = jnp.where(strict4, b64 * kkqk[c][:CHUNK] * decay, 0.0)
        in_ref[d, rows[c], :] = (kkqk[c][CHUNK:] * decay).astype(bf16)
        p[c, d] = jnp.where(ri == ci, 1.0, 0.0) - jnp.where(ri // 2 == ci // 2, a[c, d], 0.0)

    m = 2
    while m < CHUNK:
        sib = jnp.logical_and(ri // (2 * m) == ci // (2 * m), ri // m != ci // m)
        x = {}
        for i in insts:
            x[i] = _dot(p[i].astype(bf16), _block_diag(jnp.where(sib, a[i], 0.0).astype(bf16), H))
        for i in insts:
            p[i] = p[i] - _dot(x[i].astype(bf16), _block_diag(p[i].astype(bf16), H))
        m *= 2

    for c, d in insts:
        rs = rows[c]
        Gc = G[c, d]
        tb = p[c, d].astype(bf16)
        last = 0 if d else CHUNK - 1
        g_last = Gc[last:last + 1, :]
        e_g = jnp.exp(Gc)
        _, b128 = _gate_cols(g_ref[rs, :], d, DN_DK)
        k_all = k_ref[rs, :]
        vbeta = v_ref[rs, :] * b128
        kbg = k_all * b128 * e_g
        qd_ref[d, rs, :] = (q_ref[rs, :] * e_g).astype(bf16)
        kd_ref[d, rs, :] = (k_all * jnp.exp(g_last - Gc)).astype(bf16)
        el_ref[d, c] = jnp.exp(g_last)
        for h in range(H):
            sl = slice(h * DN_DK, (h + 1) * DN_DK)
            sc = slice(h * CHUNK, (h + 1) * CHUNK)
            uw = _dot(tb[:, sc], jnp.concatenate([vbeta[:, sl], kbg[:, sl]], axis=1).astype(bf16))
            u_ref[d, rs, sl] = uw[:, :DN_DV]
            w_ref[d, rs, sl] = uw[:, DN_DV:].astype(bf16)


def _delta_prep(q, k, v, gates, lay):
    R = q.shape[0]
    tm = lay.tm
    nc = tm // CHUNK
    wq = DN_H * DN_DK
    wv = DN_H * DN_DV
    wi = DN_H * CHUNK
    row = lambda i: (i, 0)
    drow = lambda i: (0, i, 0)
    body = functools.partial(_delta_prep_body, nc=nc)
    return pl.pallas_call(
        body,
        out_shape=(jax.ShapeDtypeStruct((2, R, wq), bf16), jax.ShapeDtypeStruct((2, R, wv), f32),
                   jax.ShapeDtypeStruct((2, R, wq), bf16), jax.ShapeDtypeStruct((2, R, wq), bf16),
                   jax.ShapeDtypeStruct((2, R, wi), bf16), jax.ShapeDtypeStruct((2, R // CHUNK, 1, wq), f32)),
        grid=(R // tm,),
        in_specs=[pl.BlockSpec((tm, wq), row), pl.BlockSpec((tm, wq), row), pl.BlockSpec((tm, wv), row),
                  pl.BlockSpec((tm, LANES), row)],
        out_specs=(pl.BlockSpec((2, tm, wq), drow), pl.BlockSpec((2, tm, wv), drow),
                   pl.BlockSpec((2, tm, wq), drow), pl.BlockSpec((2, tm, wq), drow),
                   pl.BlockSpec((2, tm, wi), drow), pl.BlockSpec((2, nc, 1, wq), lambda i: (0, i, 0, 0))),
        compiler_params=_cparams("parallel"),
        name="delta_prep",
    )(q, k, v, gates)


def _rec_scan_body(wf, uf, qdf, kdf, inf, elf, wb, ub, qdb, kdb, inb, elb, qif, kvf, alf, qib, kvb, alb,
                   of_ref, ob_ref, gof_ref, gob_ref, s_sc, st_sc, *, g):
    @pl.when(pl.program_id(1) == 0)
    def _():
        s_sc[...] = jnp.zeros_like(s_sc)
        st_sc[...] = jnp.zeros_like(st_sc)

    gviews = ((qif, kvf, alf, gof_ref), (qib, kvb, alb, gob_ref))
    st = [st_sc[0], st_sc[1]]
    gstate = {}
    for j in range(g):
        for d in range(2):
            c = g - 1 - j if d else j
            gstate[d, c] = st[d].astype(bf16)
            st[d] = st[d] * gviews[d][2][0, c] + gviews[d][1][0, c]
    st_sc[0] = st[0]
    st_sc[1] = st[1]

    def gla_outputs(c):
        for d in range(2):
            qi_r, o_r = gviews[d][0], gviews[d][3]
            rs = slice(c * CHUNK, (c + 1) * CHUNK)
            for h in range(GLA_H):
                sk = slice(h * GLA_DK, (h + 1) * GLA_DK)
                o_r[rs, h * GLA_DV:(h + 1) * GLA_DV] = _dot_nt(qi_r[0, rs, sk], gstate[d, c][:, sk]).astype(bf16)

    views = ((wf, uf, qdf, kdf, inf, elf, of_ref), (wb, ub, qdb, kdb, inb, elb, ob_ref))
    chains = [(d, h) for d in range(2) for h in range(DN_H)]
    S = {ch: s_sc[ch[0], ch[1]] for ch in chains}
    for j in range(g):
        r1, vnb = {}, {}
        for d, h in chains:
            w_r, u_r, qd_r = views[d][0], views[d][1], views[d][2]
            rs = slice((g - 1 - j if d else j) * CHUNK, (g - j if d else j + 1) * CHUNK)
            sl = slice(h * DN_DK, (h + 1) * DN_DK)
            r1[d, h] = _dot(jnp.concatenate([w_r[0, rs, sl], qd_r[0, rs, sl]], axis=0), S[d, h].astype(bf16))
        gla_outputs(j)
        for d, h in chains:
            u_r = views[d][1]
            rs = slice((g - 1 - j if d else j) * CHUNK, (g - j if d else j + 1) * CHUNK)
            sl = slice(h * DN_DK, (h + 1) * DN_DK)
            vnb[d, h] = (u_r[0, rs, sl] - r1[d, h][:CHUNK]).astype(bf16)
        for d, h in chains:
            kd_r, in_r, el_r, o_r = views[d][3], views[d][4], views[d][5], views[d][6]
            c = g - 1 - j if d else j
            rs = slice(c * CHUNK, (c + 1) * CHUNK)
            sl = slice(h * DN_DK, (h + 1) * DN_DK)
            sc = slice(h * CHUNK, (h + 1) * CHUNK)
            o_r[rs, sl] = (r1[d, h][CHUNK:] + _dot(in_r[0, rs, sc], vnb[d, h])).astype(bf16)
            S[d, h] = S[d, h] * el_r[0, c][:, sl] + _dot_tn(kd_r[0, rs, sl], vnb[d, h])
    for d, h in chains:
        s_sc[d, h] = S[d, h]


def _group_maps(lay, g):
    B = lay.B
    ngc, ngl = lay.Tc // (CHUNK * g), lay.T // (CHUNK * g)

    def fwd(b, n):
        return jnp.where(n < ngc, b * ngc + n, B * ngc + b * ngl + (n - ngc))

    def bwd(b, n):
        return jnp.where(n < ngc, b * ngc + (ngc - 1 - n), B * ngc + b * ngl + (ngl - 1 - (n - ngc)))

    return fwd, bwd, ngc + ngl


def _scan_group(lay):
    g = min(4, lay.Tc // CHUNK)
    assert (lay.Tc // CHUNK) % g == 0 and (lay.T // CHUNK) % g == 0
    return g


def _rec_scan(delta_in, gla_in, lay):
    w, u, qd, kd, intra, el = delta_in
    qi, kv, al = gla_in
    R = w.shape[1]
    g = _scan_group(lay)
    fwd, bwd, nstep = _group_maps(lay, g)
    wq = DN_H * DN_DK
    wv = DN_H * DN_DV
    wi = DN_H * CHUNK
    wk = GLA_H * GLA_DK
    rows = g * CHUNK

    def delta_view(m, d):
        sp = lambda width: pl.BlockSpec((1, rows, width), lambda b, n: (d, m(b, n), 0))
        return [sp(wq), sp(wv), sp(wq), sp(wq), sp(wi), pl.BlockSpec((1, g, 1, wq), lambda b, n: (d, m(b, n), 0, 0))]

    def gla_view(m, d):
        return [pl.BlockSpec((1, rows, wk), lambda b, n: (d, m(b, n), 0)),
                pl.BlockSpec((1, g, GLA_DV, wk), lambda b, n: (d, m(b, n), 0, 0)),
                pl.BlockSpec((1, g, 1, wk), lambda b, n: (d, m(b, n), 0, 0))]

    def out(m, width):
        return pl.BlockSpec((rows, width), lambda b, n: (m(b, n), 0))

    o_shape = jax.ShapeDtypeStruct((R, wv), bf16)
    go_shape = jax.ShapeDtypeStruct((R, GLA_H * GLA_DV), bf16)
    body = functools.partial(_rec_scan_body, g=g)
    return pl.pallas_call(
        body,
        out_shape=(o_shape, o_shape, go_shape, go_shape),
        grid=(lay.B, nstep),
        in_specs=delta_view(fwd, 0) + delta_view(bwd, 1) + gla_view(fwd, 0) + gla_view(bwd, 1),
        out_specs=(out(fwd, wv), out(bwd, wv), out(fwd, GLA_H * GLA_DV), out(bwd, GLA_H * GLA_DV)),
        scratch_shapes=[pltpu.VMEM((2, DN_H, DN_DK, DN_DV), f32), pltpu.VMEM((2, GLA_DV, wk), f32)],
        compiler_params=_cparams("parallel", "arbitrary"),
        name="rec_scan",
    )(w, u, qd, kd, intra, el, w, u, qd, kd, intra, el, qi, kv, al, qi, kv, al)


def _gla_prep_body(q_ref, k_ref, v_ref, la_ref, oi_ref, qi_ref, kv_ref, al_ref, *, nc):
    H = GLA_H
    insts = [(c, d) for c in range(nc) for d in range(2)]
    rows = [slice(c * CHUNK, (c + 1) * CHUNK) for c in range(nc)]
    incl4 = [_tri_masks(d == 1, H * CHUNK)[0] for d in range(2)]
    mbs = [_tri_masks(d == 1, CHUNK)[0].astype(bf16) for d in range(2)]
    b = {}
    for c, d in insts:
        hi, lo = _split2(la_ref[d, rows[c], :])
        b[c, d] = _dot(jnp.concatenate([mbs[d], mbs[d]], axis=1), jnp.concatenate([hi, lo], axis=0))
    att = {}
    for c, d in insts:
        q = q_ref[rows[c], :] * (GLA_DK ** -0.5)
        mid = CHUNK - 1 - CHUNK // 2 if d else CHUNK // 2
        b_mid = b[c, d][mid:mid + 1, :]
        qe = (q * jnp.exp(b[c, d] - b_mid)).astype(bf16)
        ke = (k_ref[rows[c], :] * jnp.exp(b_mid - b[c, d])).astype(bf16)
        att[c, d] = jnp.where(incl4[d], _dot_nt(qe, _block_diag(ke, H)), 0.0).astype(bf16)
        qi_ref[d, rows[c], :] = (q * jnp.exp(b[c, d])).astype(bf16)
    for c, d in insts:
        v_bf = v_ref[rows[c], :].astype(bf16)
        oi_ref[d, rows[c], :] = _dot(att[c, d], _block_diag(v_bf, H)).astype(bf16)
        last = 0 if d else CHUNK - 1
        b_last = b[c, d][last:last + 1, :]
        ks = (k_ref[rows[c], :] * jnp.exp(b_last - b[c, d])).astype(bf16)
        kv = [_dot_tn(v_bf[:, h * GLA_DV:(h + 1) * GLA_DV], ks[:, h * GLA_DK:(h + 1) * GLA_DK]) for h in range(H)]
        kv_ref[d, c] = jnp.concatenate(kv, axis=1)
        al_ref[d, c] = jnp.exp(b_last)


def _gla_prep(ym, la, lay):
    R = ym.shape[0]
    tm = lay.tm
    nc = tm // CHUNK
    wk = GLA_H * GLA_DK
    wv = GLA_H * GLA_DV
    cq, ck, cv = OFF_GQ // wk, OFF_GK // wk, OFF_GV // wv
    drow = lambda i: (0, i, 0)
    body = functools.partial(_gla_prep_body, nc=nc)
    return pl.pallas_call(
        body,
        out_shape=(jax.ShapeDtypeStruct((2, R, wv), bf16), jax.ShapeDtypeStruct((2, R, wk), bf16),
                   jax.ShapeDtypeStruct((2, R // CHUNK, GLA_DV, wk), f32),
                   jax.ShapeDtypeStruct((2, R // CHUNK, 1, wk), f32)),
        grid=(R // tm,),
        in_specs=[pl.BlockSpec((tm, wk), lambda i: (i, cq)), pl.BlockSpec((tm, wk), lambda i: (i, ck)),
                  pl.BlockSpec((tm, wv), lambda i: (i, cv)), pl.BlockSpec((2, tm, wk), drow)],
        out_specs=(pl.BlockSpec((2, tm, wv), drow), pl.BlockSpec((2, tm, wk), drow),
                   pl.BlockSpec((2, nc, GLA_DV, wk), lambda i: (0, i, 0, 0)),
                   pl.BlockSpec((2, nc, 1, wk), lambda i: (0, i, 0, 0))),
        compiler_params=_cparams("parallel"),
        name="gla_prep",
    )(ym, ym, ym, la)


def _recout_body(df_ref, db_ref, gf_ref, gb_ref, gi_ref, z_ref, r_ref, x_ref, m_ref, dn_ref, gn_ref, w_ref, o_ref):
    up = lambda r: r.astype(f32)
    dn = up(df_ref[...]) + up(db_ref[...])
    gl = (up(gf_ref[...]) + up(gi_ref[0])) + (up(gb_ref[...]) + up(gi_ref[1]))
    z = z_ref[...]
    r = r_ref[...]
    parts = []
    for src, gate, gain, hd, nh in ((dn, z, dn_ref[...], DN_DV, DN_H), (gl, r, gn_ref[...], GLA_DV, GLA_H)):
        for h in range(nh):
            a = src[:, h * hd:(h + 1) * hd]
            a = a * lax.rsqrt(jnp.mean(a * a, axis=-1, keepdims=True) + EPS) * gain
            parts.append(a * _silu(gate[:, h * hd:(h + 1) * hd]))
    mix = jnp.concatenate(parts, axis=1).astype(bf16)
    o_ref[...] = x_ref[...] + m_ref[0][2:3, :] * _dot(mix, w_ref[...])


def _recout(dn_f, dn_b, gl_f, gl_b, gl_i, ym, xs, mods_l, dn_norm, gla_norm, w, lay):
    R, D = xs.shape
    tm = lay.tmf
    wd = DN_H * DN_DV
    wg = GLA_H * GLA_DV
    row = lambda i: (i, 0)
    return pl.pallas_call(
        _recout_body,
        out_shape=jax.ShapeDtypeStruct((R, D), f32),
        grid=(R // tm,),
        in_specs=[pl.BlockSpec((tm, wd), row), pl.BlockSpec((tm, wd), row),
                  pl.BlockSpec((tm, wg), row), pl.BlockSpec((tm, wg), row),
                  pl.BlockSpec((2, tm, wg), lambda i: (0, i, 0)),
                  pl.BlockSpec((tm, wd), lambda i: (i, OFF_DZ // wd)),
                  pl.BlockSpec((tm, wg), lambda i: (i, OFF_GR // wg)),
                  pl.BlockSpec((tm, D), row),
                  pl.BlockSpec((1, 6, D), lambda i: (lay.mod_row(i * tm), 0, 0)),
                  _const_spec(dn_norm.shape), _const_spec(gla_norm.shape), _const_spec(w.shape)],
        out_specs=pl.BlockSpec((tm, D), row),
        compiler_params=_cparams("parallel"),
        name="rec_outproj",
    )(dn_f, dn_b, gl_f, gl_b, gl_i, ym, ym, xs, mods_l, dn_norm, gla_norm, w)


def _rec_weights(w_in):
    sizes = (DN_H * DN_DK, DN_H * DN_DK, DN_H * DN_DV, DN_H * DN_DV, 2 * DN_H, 2 * DN_H,
             GLA_H * GLA_DK, GLA_H * GLA_DK, GLA_H * GLA_DV, GLA_H * GLA_DV, 2 * GLA_RANK)
    offs = [0]
    for s in sizes:
        offs.append(offs[-1] + s)
    seg = [w_in[:, offs[i]:offs[i + 1]] for i in range(len(sizes))]
    dq, dk, dv, dz, da, db, gq, gk, gv, gr, gg = seg
    main = jnp.concatenate([dq, dk, dv, dz, gq, gk, gv, gr], axis=1).astype(bf16)
    small = jnp.concatenate([da, db, gg], axis=1)
    small = jnp.pad(small, ((0, 0), (0, LANES - small.shape[1]))).astype(bf16)
    return main, small


def _pad_lanes(v, off=0):
    v = v.reshape(1, -1)
    return jnp.pad(v, ((0, 0), (off, LANES - off - v.shape[1])))


def _gla_gate_weights(w2):
    out = jnp.zeros((2, LANES, w2.shape[2]), f32)
    for d in range(2):
        out = out.at[d, SM_GG + d * GLA_RANK:SM_GG + (d + 1) * GLA_RANK, :].set(w2[d])
    hi = out.astype(bf16)
    lo = (out - hi.astype(f32)).astype(bf16)
    return hi, lo


def _ffn_weights(w_up, conv, w_down):
    D, F2 = w_up.shape
    F = F2 // 2
    n = F // FFN_CHUNK
    wg = w_up[:, :F].astype(bf16).reshape(D, n, FFN_CHUNK).transpose(1, 0, 2)
    wv = w_up[:, F:].astype(bf16).reshape(D, n, FFN_CHUNK).transpose(1, 0, 2)
    cw = conv.reshape(3, n, FFN_CHUNK).transpose(1, 0, 2)
    wd = w_down.astype(bf16).reshape(n, FFN_CHUNK, D)
    return wg, wv, cw, wd


def _rope_tables(lay):
    T, Tc, B = lay.T, lay.Tc, lay.B
    t = jnp.arange(T, dtype=jnp.int32)
    row = (t // GRID_W).astype(f32)
    col = (t % GRID_W).astype(f32)
    inv_freq = ROPE_THETA ** (-jnp.arange(ROPE_PAIRS, dtype=f32) / ROPE_PAIRS)
    ar = row[:, None] * inv_freq
    ac = col[:, None] * inv_freq
    cos = jnp.concatenate([jnp.cos(ar), jnp.cos(ar), jnp.cos(ac), jnp.cos(ac)], axis=1)
    sin = jnp.concatenate([-jnp.sin(ar), jnp.sin(ar), -jnp.sin(ac), jnp.sin(ac)], axis=1)
    cos = jnp.concatenate([jnp.ones((B * Tc, ATT_HD), f32)] + [cos] * B, axis=0)
    sin = jnp.concatenate([jnp.zeros((B * Tc, ATT_HD), f32)] + [sin] * B, axis=0)
    return cos, sin


def kernel(x, c, ctx, c_ctx, mod_w, mod_b, rec_w_in, rec_conv, dn_a_log, dn_dt_bias, dn_norm, gla_w2, gla_b2,
           gla_norm, rec_w_out, att_w_qkv, att_q_norm, att_k_norm, att_w_out, ffn_w_up, ffn_conv, ffn_w_down,
           final_norm):
    B, T, D = x.shape
    Tc = ctx.shape[1]
    depth = mod_w.shape[0]
    lay = _Layout(B, T, Tc)
    assert B + 1 <= MOD_ROWS
    xs = jnp.concatenate([ctx.reshape(B * Tc, D), x.reshape(B * T, D)], axis=0)
    cond = jnp.concatenate([c, c_ctx[None], jnp.zeros((MOD_ROWS - B - 1, D), f32)], axis=0)
    mods = _mods(cond, mod_w, mod_b)
    cos, sin = _rope_tables(lay)

    for i in range(depth):
        ml = mods[i]
        if i % 2 == 0:
            e = i // 2
            wm, ws = _rec_weights(rec_w_in[e])
            ym, ys = _recin(xs, ml, wm, ws, lay)
            w2h, w2l = _gla_gate_weights(gla_w2[e])
            q, k, v, gates, la = _recprep(
                ym, ys, rec_conv[e], _pad_lanes(dn_a_log[e], SM_DA), _pad_lanes(dn_dt_bias[e], SM_DA),
                w2h, w2l, gla_b2[e].reshape(2, 1, -1), lay)
            gl_i, qi, kv, al = _gla_prep(ym, la, lay)
            dn_f, dn_b, gl_f, gl_b = _rec_scan(_delta_prep(q, k, v, gates, lay), (qi, kv, al), lay)
            xs = _recout(dn_f, dn_b, gl_f, gl_b, gl_i, ym, xs, ml, dn_norm[e].reshape(1, -1),
                         gla_norm[e].reshape(1, -1), rec_w_out[e].astype(bf16), lay)
        else:
            o = i // 2
            q, k, v = _qkv(xs, ml, att_w_qkv[o].astype(bf16), att_q_norm[o].reshape(1, -1),
                           att_k_norm[o].reshape(1, -1), cos, sin, lay)
            a = _flash(q, k, v, lay)
            xs = _outproj(a, xs, ml, att_w_out[o].astype(bf16), lay)
        wg, wv, cw, wd = _ffn_weights(ffn_w_up[i], ffn_conv[i], ffn_w_down[i])
        last = i == depth - 1
        xs = _ffn(xs, ml, wg, wv, cw, wd, lay, final_gain=final_norm.reshape(1, -1) if last else None)

    return xs.reshape(B, T, D)
```

```python
import functools
import math

import jax
import jax.numpy as jnp
from jax import lax
from jax.experimental import pallas as pl
from jax.experimental.pallas import tpu as pltpu

f32 = jnp.float32
bf16 = jnp.bfloat16

EPS = 1e-6
GRID_W = 64
DN_H = 4
DN_DK = 128
DN_DV = 128
GLA_H = 4
GLA_DK = 64
GLA_DV = 128
GLA_RANK = 16
GLA_TAU = 16.0
CHUNK = 64
ATT_H = 8
ATT_KVH = 2
ATT_G = ATT_H // ATT_KVH
ATT_HD = 128
ROPE_THETA = 10000.0
ROPE_PAIRS = ATT_HD // 4
LOG2E = math.log2(math.e)
FFN_CHUNK = 256
MOD_ROWS = 8
LANES = 128
VMEM_LIMIT = 56 * 1024 * 1024


def _cparams(*sem):
    return pltpu.CompilerParams(dimension_semantics=sem, vmem_limit_bytes=VMEM_LIMIT)


def _const_spec(shape):
    nd = len(shape)
    return pl.BlockSpec(shape, lambda *_: (0,) * nd, pipeline_mode=pl.Buffered(1))


class _Layout:
    def __init__(self, B, T, Tc):
        self.B, self.T, self.Tc = B, T, Tc
        self.R = B * Tc + B * T
        self.tm = min(256, Tc)
        assert Tc % self.tm == 0 and T % self.tm == 0 and T % CHUNK == 0 and Tc % CHUNK == 0
        self.tmf = min(512, B * Tc)
        assert (B * Tc) % self.tmf == 0 and T % self.tmf == 0
        self.seq_starts = [b * Tc for b in range(B)] + [B * Tc + b * T for b in range(B)]

    def mod_row(self, row0):
        return jnp.where(row0 < self.B * self.Tc, self.B, (row0 - self.B * self.Tc) // self.T)

    def is_seq_start(self, row0):
        return jnp.where(row0 < self.B * self.Tc, row0 % self.Tc == 0, (row0 - self.B * self.Tc) % self.T == 0)

    def edge_masks(self, rows):
        first = rows == self.seq_starts[0]
        last = rows == self.R - 1
        for s in self.seq_starts[1:]:
            first = jnp.logical_or(first, rows == s)
            last = jnp.logical_or(last, rows == s - 1)
        return first, last


def _normmod(x, m, shift_i, scale_i):
    ms = jnp.mean(x * x, axis=-1, keepdims=True)
    return (x * lax.rsqrt(ms + EPS)) * (1.0 + m[scale_i:scale_i + 1, :]) + m[shift_i:shift_i + 1, :]


def _silu(x):
    return x * jax.nn.sigmoid(x)


def _split2(x):
    hi = x.astype(bf16)
    lo = (x - hi.astype(f32)).astype(bf16)
    return hi, lo


def _dot(a, b):
    return jnp.dot(a, b, preferred_element_type=f32)


def _dot_nt(a, b):
    return lax.dot_general(a, b, (((1,), (1,)), ((), ())), preferred_element_type=f32)


def _dot_tn(a, b):
    return lax.dot_general(a, b, (((0,), (0,)), ((), ())), preferred_element_type=f32)


def _mod_body(c_ref, w_ref, b_ref, o_ref):
    a = _silu(c_ref[...]).astype(bf16)
    o_ref[0] = _dot(a, w_ref[0].astype(bf16)) + b_ref[0]


def _mods(cond, mod_w, mod_b):
    depth, D, D6 = mod_w.shape
    tn = 1536
    out = pl.pallas_call(
        _mod_body,
        out_shape=jax.ShapeDtypeStruct((depth, MOD_ROWS, D6), f32),
        grid=(depth, D6 // tn),
        in_specs=[pl.BlockSpec((MOD_ROWS, D), lambda l, j: (0, 0)),
                  pl.BlockSpec((1, D, tn), lambda l, j: (l, 0, j)),
                  pl.BlockSpec((1, 1, tn), lambda l, j: (l, 0, j))],
        out_specs=pl.BlockSpec((1, MOD_ROWS, tn), lambda l, j: (l, 0, j)),
        compiler_params=_cparams("parallel", "parallel"),
        name="mods",
    )(cond, mod_w, mod_b.reshape(depth, 1, D6))
    return out.reshape(depth, MOD_ROWS, 6, D)


def _ffn_body(xp_ref, x_ref, xn_ref, m_ref, wu_ref, cw_ref, wd_ref, *rest, lay, halo, off, final):
    fg_ref, o_ref, h_sc, g_sc, acc_sc = rest if final else (None,) + rest
    tm = lay.tmf
    F = wd_ref.shape[0]
    row0 = (pl.program_id(0) + off) * tm
    m = m_ref[0]
    x = x_ref[...]
    xe = jnp.concatenate([xp_ref[...], x, xn_ref[...]], axis=0)
    h_sc[...] = _normmod(xe, m, 3, 4).astype(bf16)
    kill_prev, kill_next = lay.edge_masks(row0 + lax.broadcasted_iota(jnp.int32, (tm, 1), 0))
    acc_sc[...] = jnp.zeros_like(acc_sc)

    for c0 in range(0, F, FFN_CHUNK):
        cs = slice(c0, c0 + FFN_CHUNK)
        g_sc[...] = _dot(h_sc[...], wu_ref[:, cs])
        val = _dot(h_sc[pl.ds(halo, tm), :], wu_ref[:, F + c0:F + c0 + FFN_CHUNK])
        gm1 = jnp.where(kill_prev, 0.0, g_sc[pl.ds(halo - 1, tm), :])
        g0 = g_sc[pl.ds(halo, tm), :]
        gp1 = jnp.where(kill_next, 0.0, g_sc[pl.ds(halo + 1, tm), :])
        gate = gm1 * cw_ref[0:1, cs] + g0 * cw_ref[1:2, cs] + gp1 * cw_ref[2:3, cs]
        act = (_silu(gate) * val).astype(bf16)
        acc_sc[...] += _dot(act, wd_ref[cs, :])
    y = x + m[5:6, :] * acc_sc[...]
    if final:
        y = y * lax.rsqrt(jnp.mean(y * y, axis=-1, keepdims=True) + EPS) * fg_ref[...]
    o_ref[...] = y


def _ffn(xs, mods_l, wu, cw, wd, lay, final_gain=None):
    R, D = xs.shape
    tm = lay.tmf
    halo = 16
    cf = FFN_CHUNK
    assert wd.shape[0] % cf == 0 and wu.shape[1] == 2 * wd.shape[0]
    nh = tm // halo
    nhb = R // halo
    final = final_gain is not None
    off = lay.B * lay.Tc // tm if final else 0
    body = functools.partial(_ffn_body, lay=lay, halo=halo, off=off, final=final)
    extra_specs = [_const_spec(final_gain.shape)] if final else []
    extra_args = [final_gain] if final else []
    return pl.pallas_call(
        body,
        out_shape=jax.ShapeDtypeStruct((R - off * tm, D), f32),
        grid=(R // tm - off,),
        in_specs=[pl.BlockSpec((halo, D), lambda i: (jnp.maximum((i + off) * nh - 1, 0), 0)),
                  pl.BlockSpec((tm, D), lambda i: (i + off, 0)),
                  pl.BlockSpec((halo, D), lambda i: (jnp.minimum((i + off + 1) * nh, nhb - 1), 0)),
                  pl.BlockSpec((1, 6, D), lambda i: (lay.mod_row((i + off) * tm), 0, 0)),
                  _const_spec(wu.shape), _const_spec(cw.shape), _const_spec(wd.shape)]
        + extra_specs,
        out_specs=pl.BlockSpec((tm, D), lambda i: (i, 0)),
        scratch_shapes=[pltpu.VMEM((tm + 2 * halo, D), bf16),
                        pltpu.VMEM((tm + 2 * halo, cf), f32),
                        pltpu.VMEM((tm, D), f32)],
        compiler_params=_cparams("parallel"),
        name="conv_ffn",
    )(xs, xs, xs, mods_l, wu, cw, wd, *extra_args)


def _qkv_body(x_ref, m_ref, w_ref, qn_ref, kn_ref, cos_ref, sin_ref, q_ref, k_ref, v_ref):
    h = _normmod(x_ref[...], m_ref[0], 0, 1).astype(bf16)
    y = _dot(h, w_ref[...])
    cos = cos_ref[...]
    sin = sin_ref[...]
    lane = lax.broadcasted_iota(jnp.int32, cos.shape, 1)
    first = (lane % (2 * ROPE_PAIRS)) < ROPE_PAIRS

    def norm_rope(z, gain):
        z = z * lax.rsqrt(jnp.mean(z * z, axis=-1, keepdims=True) + EPS) * gain
        partner = jnp.where(first, pltpu.roll(z, LANES - ROPE_PAIRS, axis=1), pltpu.roll(z, ROPE_PAIRS, axis=1))
        return z * cos + partner * sin

    nq = ATT_H * ATT_HD
    for hh in range(ATT_H):
        z = norm_rope(y[:, hh * ATT_HD:(hh + 1) * ATT_HD], qn_ref[...])
        q_ref[:, hh * ATT_HD:(hh + 1) * ATT_HD] = (z * (ATT_HD ** -0.5 * LOG2E)).astype(bf16)
    for kk in range(ATT_KVH):
        z = norm_rope(y[:, nq + kk * ATT_HD:nq + (kk + 1) * ATT_HD], kn_ref[...])
        k_ref[:, kk * ATT_HD:(kk + 1) * ATT_HD] = z.astype(bf16)
    nk = nq + ATT_KVH * ATT_HD
    ones = jnp.ones((y.shape[0], ATT_HD), bf16)
    for kk in range(ATT_KVH):
        v_ref[:, 2 * kk * ATT_HD:(2 * kk + 1) * ATT_HD] = y[:, nk + kk * ATT_HD:nk + (kk + 1) * ATT_HD].astype(bf16)
        v_ref[:, (2 * kk + 1) * ATT_HD:(2 * kk + 2) * ATT_HD] = ones


def _qkv(xs, mods_l, w, qn, kn, cos, sin, lay):
    R, D = xs.shape
    tm = lay.tmf
    nq = ATT_H * ATT_HD
    nkv = ATT_KVH * ATT_HD
    return pl.pallas_call(
        _qkv_body,
        out_shape=(jax.ShapeDtypeStruct((R, nq), bf16), jax.ShapeDtypeStruct((R, nkv), bf16),
                   jax.ShapeDtypeStruct((R, 2 * nkv), bf16)),
        grid=(R // tm,),
        in_specs=[pl.BlockSpec((tm, D), lambda i: (i, 0)),
                  pl.BlockSpec((1, 6, D), lambda i: (lay.mod_row(i * tm), 0, 0)),
                  _const_spec(w.shape), _const_spec(qn.shape), _const_spec(kn.shape),
                  pl.BlockSpec((tm, ATT_HD), lambda i: (i, 0)),
                  pl.BlockSpec((tm, ATT_HD), lambda i: (i, 0))],
        out_specs=(pl.BlockSpec((tm, nq), lambda i: (i, 0)), pl.BlockSpec((tm, nkv), lambda i: (i, 0)),
                   pl.BlockSpec((tm, 2 * nkv), lambda i: (i, 0))),
        compiler_params=_cparams("parallel"),
        name="qkv_rope",
    )(xs, mods_l, w, qn, kn, cos, sin)


def _flash_body(q_ref, kc_ref, vc_ref, kl_ref, vl_ref, o_ref, m_sc, acc_sc, *, tq, ncq, ck):
    qi = pl.program_id(2)
    q = q_ref[...]
    q4 = jnp.concatenate([q[:, g * ATT_HD:(g + 1) * ATT_HD] for g in range(ATT_G)], axis=0)
    m_sc[...] = jnp.full_like(m_sc, -jnp.inf)
    acc_sc[...] = jnp.zeros_like(acc_sc)

    def chunk(s, v):
        m_prev = m_sc[...]
        m_new = jnp.maximum(m_prev, jnp.max(s, axis=-1, keepdims=True))
        p = jnp.exp2(s - jnp.tile(m_new, (1, s.shape[1] // LANES)))
        alpha = jnp.exp2(m_prev - m_new)
        acc_sc[...] = jnp.tile(alpha, (1, 2)) * acc_sc[...] + _dot(p.astype(bf16), v)
        m_sc[...] = m_new

    chunk(_dot_nt(q4, kc_ref[...]), vc_ref[...])

    @pl.when(qi >= ncq)
    def _():
        n = kl_ref.shape[0] // ck
        s_next = _dot_nt(q4, kl_ref[0:ck, :])
        for c in range(n):
            s = s_next
            if c + 1 < n:
                s_next = _dot_nt(q4, kl_ref[(c + 1) * ck:(c + 2) * ck, :])
            chunk(s, vl_ref[c * ck:(c + 1) * ck, :])

    acc = acc_sc[...]
    out = acc[:, :ATT_HD] / acc[:, ATT_HD:]
    for g in range(ATT_G):
        o_ref[:, g * ATT_HD:(g + 1) * ATT_HD] = out[g * tq:(g + 1) * tq, :].astype(bf16)


def _flash(q, k, v1, lay):
    R = q.shape[0]
    B, T, Tc = lay.B, lay.T, lay.Tc
    tq = lay.tm
    ncq, nlq = Tc // tq, T // tq
    gw = ATT_G * ATT_HD
    ck = min(2048, T)
    assert T % ck == 0
    kc, kl = k[:B * Tc], k[B * Tc:]
    vc, vl = v1[:B * Tc], v1[B * Tc:]

    def q_map(b, kv, qi):
        return (jnp.where(qi < ncq, b * ncq + qi, B * ncq + b * nlq + (qi - ncq)), kv)

    kv_map = lambda b, kv, qi: (b, kv)
    body = functools.partial(_flash_body, tq=tq, ncq=ncq, ck=ck)
    return pl.pallas_call(
        body,
        out_shape=jax.ShapeDtypeStruct((R, ATT_H * ATT_HD), bf16),
        grid=(B, ATT_KVH, ncq + nlq),
        in_specs=[pl.BlockSpec((tq, gw), q_map),
                  pl.BlockSpec((Tc, ATT_HD), kv_map), pl.BlockSpec((Tc, 2 * ATT_HD), kv_map),
                  pl.BlockSpec((T, ATT_HD), kv_map), pl.BlockSpec((T, 2 * ATT_HD), kv_map)],
        out_specs=pl.BlockSpec((tq, gw), q_map),
        scratch_shapes=[pltpu.VMEM((ATT_G * tq, LANES), f32), pltpu.VMEM((ATT_G * tq, 2 * ATT_HD), f32)],
        compiler_params=_cparams("parallel", "parallel", "parallel"),
        name="flash_gqa",
    )(q, kc, vc, kl, vl)


def _outproj_body(a_ref, x_ref, m_ref, w_ref, o_ref):
    o_ref[...] = x_ref[...] + m_ref[0][2:3, :] * _dot(a_ref[...], w_ref[...])


def _outproj(a, xs, mods_l, w, lay):
    R, D = xs.shape
    tm = lay.tmf
    K = a.shape[1]
    return pl.pallas_call(
        _outproj_body,
        out_shape=jax.ShapeDtypeStruct((R, D), f32),
        grid=(R // tm,),
        in_specs=[pl.BlockSpec((tm, K), lambda i: (i, 0)),
                  pl.BlockSpec((tm, D), lambda i: (i, 0)),
                  pl.BlockSpec((1, 6, D), lambda i: (lay.mod_row(i * tm), 0, 0)),
                  _const_spec(w.shape)],
        out_specs=pl.BlockSpec((tm, D), lambda i: (i, 0)),
        compiler_params=_cparams("parallel"),
        name="att_outproj",
    )(a, xs, mods_l, w)


REC_MAIN = 2 * DN_H * DN_DK + 2 * DN_H * DN_DV + 2 * GLA_H * GLA_DK + 2 * GLA_H * GLA_DV
REC_QKV = 2 * DN_H * DN_DK + DN_H * DN_DV
OFF_DZ = REC_QKV
OFF_GQ = OFF_DZ + DN_H * DN_DV
OFF_GK = OFF_GQ + GLA_H * GLA_DK
OFF_GV = OFF_GK + GLA_H * GLA_DK
OFF_GR = OFF_GV + GLA_H * GLA_DV
SM_DA = 0
SM_DB = 2 * DN_H
SM_GG = 4 * DN_H


def _recin_body(x_ref, m_ref, wm_ref, ws_ref, ym_ref, ys_ref):
    h = _normmod(x_ref[...], m_ref[0], 0, 1).astype(bf16)
    ym_ref[...] = _dot(h, wm_ref[...]).astype(bf16)
    ys_ref[...] = _dot(h, ws_ref[...])


def _recin(xs, mods_l, wm, ws, lay):
    R, D = xs.shape
    tm = lay.tmf
    return pl.pallas_call(
        _recin_body,
        out_shape=(jax.ShapeDtypeStruct((R, REC_MAIN), bf16), jax.ShapeDtypeStruct((R, LANES), f32)),
        grid=(R // tm,),
        in_specs=[pl.BlockSpec((tm, D), lambda i: (i, 0)),
                  pl.BlockSpec((1, 6, D), lambda i: (lay.mod_row(i * tm), 0, 0)),
                  _const_spec(wm.shape), _const_spec(ws.shape)],
        out_specs=(pl.BlockSpec((tm, REC_MAIN), lambda i: (i, 0)), pl.BlockSpec((tm, LANES), lambda i: (i, 0))),
        compiler_params=_cparams("parallel"),
        name="rec_inproj",
    )(xs, mods_l, wm, ws)


def _recprep_body(yp_ref, y_ref, yn_ref, ys_ref, cw_ref, av_ref, dtb_ref, w2h_ref, w2l_ref, b2_ref,
                  q_ref, k_ref, v_ref, g_ref, la_ref, e_sc, *, lay, halo):
    tm = lay.tm
    row0 = pl.program_id(0) * tm
    e_sc[0:halo, :] = yp_ref[...].astype(f32)
    e_sc[halo:halo + tm, :] = y_ref[...].astype(f32)
    e_sc[halo + tm:halo + tm + halo, :] = yn_ref[...].astype(f32)
    rowi = lax.broadcasted_iota(jnp.int32, (tm, 1), 0)
    kill_prev = jnp.logical_and(rowi == 0, lay.is_seq_start(row0))
    kill_next = jnp.logical_and(rowi == tm - 1, jnp.logical_or(lay.is_seq_start(row0 + tm), row0 + tm == lay.R))
    cw = cw_ref[...]
    gm1 = jnp.where(kill_prev, 0.0, e_sc[pl.ds(halo - 1, tm), :])
    g0 = e_sc[pl.ds(halo, tm), :]
    gp1 = jnp.where(kill_next, 0.0, e_sc[pl.ds(halo + 1, tm), :])
    s = _silu(gm1 * cw[0:1, :] + g0 * cw[1:2, :] + gp1 * cw[2:3, :])
    nqk = DN_H * DN_DK
    for h in range(DN_H):
        z = s[:, h * DN_DK:(h + 1) * DN_DK]
        qn = z * lax.rsqrt(jnp.sum(z * z, axis=-1, keepdims=True) + EPS) * (DN_DK ** -0.5)
        q_ref[:, h * DN_DK:(h + 1) * DN_DK] = qn.astype(bf16)
        z = s[:, nqk + h * DN_DK:nqk + (h + 1) * DN_DK]
        k_ref[:, h * DN_DK:(h + 1) * DN_DK] = (z * lax.rsqrt(jnp.sum(z * z, axis=-1, keepdims=True) + EPS)).astype(bf16)
    v_ref[...] = s[:, 2 * nqk:].astype(bf16)
    z = ys_ref[...]
    lane = lax.broadcasted_iota(jnp.int32, z.shape, 1)
    g = -jnp.exp(av_ref[...]) * jax.nn.softplus(z + dtb_ref[...])
    beta = jax.nn.sigmoid(z)
    g_ref[...] = jnp.where(lane < SM_DB, g, jnp.where(lane < SM_GG, beta, 0.0))
    zh, zl = _split2(z)
    for d in range(2):
        la = _dot(zh, w2h_ref[d]) + _dot(zl, w2h_ref[d]) + _dot(zh, w2l_ref[d]) + b2_ref[d]
        la_ref[d] = jax.nn.log_sigmoid(la) / GLA_TAU


def _recprep(ym, ys, cw, av, dtb, w2h, w2l, b2, lay):
    R = ym.shape[0]
    tm = lay.tm
    halo = 16
    nh = tm // halo
    nhb = R // halo
    nd = DN_H * DN_DK
    body = functools.partial(_recprep_body, lay=lay, halo=halo)
    return pl.pallas_call(
        body,
        out_shape=(jax.ShapeDtypeStruct((R, nd), bf16), jax.ShapeDtypeStruct((R, nd), bf16),
                   jax.ShapeDtypeStruct((R, DN_H * DN_DV), bf16), jax.ShapeDtypeStruct((R, LANES), f32),
                   jax.ShapeDtypeStruct((2, R, GLA_H * GLA_DK), f32)),
        grid=(R // tm,),
        in_specs=[pl.BlockSpec((halo, REC_QKV), lambda i: (jnp.maximum(i * nh - 1, 0), 0)),
                  pl.BlockSpec((tm, REC_QKV), lambda i: (i, 0)),
                  pl.BlockSpec((halo, REC_QKV), lambda i: (jnp.minimum((i + 1) * nh, nhb - 1), 0)),
                  pl.BlockSpec((tm, LANES), lambda i: (i, 0)),
                  _const_spec(cw.shape), _const_spec(av.shape), _const_spec(dtb.shape),
                  _const_spec(w2h.shape), _const_spec(w2l.shape), _const_spec(b2.shape)],
        out_specs=(pl.BlockSpec((tm, nd), lambda i: (i, 0)), pl.BlockSpec((tm, nd), lambda i: (i, 0)),
                   pl.BlockSpec((tm, DN_H * DN_DV), lambda i: (i, 0)), pl.BlockSpec((tm, LANES), lambda i: (i, 0)),
                   pl.BlockSpec((2, tm, GLA_H * GLA_DK), lambda i: (0, i, 0))),
        scratch_shapes=[pltpu.VMEM((tm + 2 * halo, REC_QKV), f32)],
        compiler_params=_cparams("parallel"),
        name="rec_prep",
    )(ym, ym, ym, ys, cw, av, dtb, w2h, w2l, b2)


def _tri_masks(rev, width):
    ri = lax.broadcasted_iota(jnp.int32, (CHUNK, width), 0)
    ci = lax.broadcasted_iota(jnp.int32, (CHUNK, width), 1) % CHUNK
    if rev:
        return ri <= ci, ri < ci, ri >= ci
    return ri >= ci, ri > ci, ri <= ci


def _block_diag(x, nblk):
    w = x.shape[1]
    t = jnp.concatenate([x] * nblk, axis=0)
    rb = lax.broadcasted_iota(jnp.int32, t.shape, 0) // CHUNK
    cb = lax.broadcasted_iota(jnp.int32, t.shape, 1) // (w // nblk)
    return jnp.where(rb == cb, t, jnp.zeros_like(t))


def _gate_cols(gt, d, width):
    H = DN_H
    g = [jnp.broadcast_to(gt[:, SM_DA + d * H + h:SM_DA + d * H + h + 1], (CHUNK, width)) for h in range(H)]
    b = [jnp.broadcast_to(gt[:, SM_DB + d * H + h:SM_DB + d * H + h + 1], (CHUNK, width)) for h in range(H)]
    return jnp.concatenate(g, axis=1), jnp.concatenate(b, axis=1)


def _delta_prep_body(q_ref, k_ref, v_ref, g_ref, w_ref, u_ref, qd_ref, kd_ref, in_ref, el_ref, *, nc):
    H = DN_H
    insts = [(c, d) for c in range(nc) for d in range(2)]
    rows = [slice(c * CHUNK, (c + 1) * CHUNK) for c in range(nc)]
    masks = [_tri_masks(d == 1, H * CHUNK) for d in range(2)]
    mbs = [_tri_masks(d == 1, CHUNK)[0].astype(bf16) for d in range(2)]
    neg1 = jnp.full((CHUNK, CHUNK), -1.0, bf16)
    ri = lax.broadcasted_iota(jnp.int32, (CHUNK, H * CHUNK), 0)
    ci = lax.broadcasted_iota(jnp.int32, (CHUNK, H * CHUNK), 1) % CHUNK

    kkqk = []
    for c in range(nc):
        k_bf = k_ref[rows[c], :]
        kq = jnp.concatenate([k_bf, q_ref[rows[c], :]], axis=0)
        kkqk.append(_dot_nt(kq, _block_diag(k_bf, H)))

    G, diff = {}, {}
    for c, d in insts:
        gt = g_ref[rows[c], :]
        incl_t4 = masks[d][2]
        g128, _ = _gate_cols(gt, d, DN_DK)
        g64, _ = _gate_cols(gt, d, CHUNK)
        hi, lo = _split2(g128)
        G[c, d] = _dot(jnp.concatenate([mbs[d], mbs[d]], axis=1), jnp.concatenate([hi, lo], axis=0))
        hi6, lo6 = _split2(g64)
        zero = jnp.zeros_like(hi6)
        rhs = jnp.concatenate([hi6, lo6, jnp.where(incl_t4, hi6, zero), jnp.where(incl_t4, lo6, zero)], axis=0)
        diff[c, d] = _dot(jnp.concatenate([mbs[d], mbs[d], neg1, neg1], axis=1), rhs)

    a, p = {}, {}
    for c, d in insts:
        incl4, strict4, _ = masks[d]
        decay = jnp.where(incl4, jnp.exp(diff[c, d]), 0.0)
        _, b64 = _gate_cols(g_ref[rows[c], :], d, CHUNK)
        a[c, d] = jnp.where(strict4, b64 * kkqk[c][:CHUNK] * decay, 0.0)
        in_ref[d, rows[c], :] = (kkqk[c][CHUNK:] * decay).astype(bf16)
        p[c, d] = jnp.where(ri == ci, 1.0, 0.0) - jnp.where(ri // 2 == ci // 2, a[c, d], 0.0)

    m = 2
    while m < CHUNK:
        sib = jnp.logical_and(ri // (2 * m) == ci // (2 * m), ri // m != ci // m)
        x = {}
        for i in insts:
            x[i] = _dot(p[i].astype(bf16), _block_diag(jnp.where(sib, a[i], 0.0).astype(bf16), H))
        for i in insts:
            p[i] = p[i] - _dot(x[i].astype(bf16), _block_diag(p[i].astype(bf16), H))
        m *= 2

    for c, d in insts:
        rs = rows[c]
        Gc = G[c, d]
        tb = p[c, d].astype(bf16)
        last = 0 if d else CHUNK - 1
        g_last = Gc[last:last + 1, :]
        e_g = jnp.exp(Gc)
        _, b128 = _gate_cols(g_ref[rs, :], d, DN_DK)
        k_all = k_ref[rs, :]
        vbeta = v_ref[rs, :] * b128
        kbg = k_all * b128 * e_g
        qd_ref[d, rs, :] = (q_ref[rs, :] * e_g).astype(bf16)
        kd_ref[d, rs, :] = (k_all * jnp.exp(g_last - Gc)).astype(bf16)
        el_ref[d, c] = jnp.exp(g_last)
        for h in range(H):
            sl = slice(h * DN_DK, (h + 1) * DN_DK)
            sc = slice(h * CHUNK, (h + 1) * CHUNK)
            uw = _dot(tb[:, sc], jnp.concatenate([vbeta[:, sl], kbg[:, sl]], axis=1).astype(bf16))
            u_ref[d, rs, sl] = uw[:, :DN_DV]
            w_ref[d, rs, sl] = uw[:, DN_DV:].astype(bf16)


def _delta_prep(q, k, v, gates, lay):
    R = q.shape[0]
    tm = lay.tm
    nc = tm // CHUNK
    wq = DN_H * DN_DK
    wv = DN_H * DN_DV
    wi = DN_H * CHUNK
    row = lambda i: (i, 0)
    drow = lambda i: (0, i, 0)
    body = functools.partial(_delta_prep_body, nc=nc)
    return pl.pallas_call(
        body,
        out_shape=(jax.ShapeDtypeStruct((2, R, wq), bf16), jax.ShapeDtypeStruct((2, R, wv), f32),
                   jax.ShapeDtypeStruct((2, R, wq), bf16), jax.ShapeDtypeStruct((2, R, wq), bf16),
                   jax.ShapeDtypeStruct((2, R, wi), bf16), jax.ShapeDtypeStruct((2, R // CHUNK, 1, wq), f32)),
        grid=(R // tm,),
        in_specs=[pl.BlockSpec((tm, wq), row), pl.BlockSpec((tm, wq), row), pl.BlockSpec((tm, wv), row),
                  pl.BlockSpec((tm, LANES), row)],
        out_specs=(pl.BlockSpec((2, tm, wq), drow), pl.BlockSpec((2, tm, wv), drow),
                   pl.BlockSpec((2, tm, wq), drow), pl.BlockSpec((2, tm, wq), drow),
                   pl.BlockSpec((2, tm, wi), drow), pl.BlockSpec((2, nc, 1, wq), lambda i: (0, i, 0, 0))),
        compiler_params=_cparams("parallel"),
        name="delta_prep",
    )(q, k, v, gates)


def _rec_scan_body(wf, uf, qdf, kdf, inf, elf, wb, ub, qdb, kdb, inb, elb, qif, kvf, alf, qib, kvb, alb,
                   of_ref, ob_ref, gof_ref, gob_ref, s_sc, st_sc, *, g):
    @pl.when(pl.program_id(1) == 0)
    def _():
        s_sc[...] = jnp.zeros_like(s_sc)
        st_sc[...] = jnp.zeros_like(st_sc)

    gviews = ((qif, kvf, alf, gof_ref), (qib, kvb, alb, gob_ref))
    st = [st_sc[0], st_sc[1]]
    gstate = {}
    for j in range(g):
        for d in range(2):
            c = g - 1 - j if d else j
            gstate[d, c] = st[d].astype(bf16)
            st[d] = st[d] * gviews[d][2][0, c] + gviews[d][1][0, c]
    st_sc[0] = st[0]
    st_sc[1] = st[1]

    def gla_outputs(c):
        for d in range(2):
            qi_r, o_r = gviews[d][0], gviews[d][3]
            rs = slice(c * CHUNK, (c + 1) * CHUNK)
            for h in range(GLA_H):
                sk = slice(h * GLA_DK, (h + 1) * GLA_DK)
                o_r[rs, h * GLA_DV:(h + 1) * GLA_DV] = _dot_nt(qi_r[0, rs, sk], gstate[d, c][:, sk]).astype(bf16)

    views = ((wf, uf, qdf, kdf, inf, elf, of_ref), (wb, ub, qdb, kdb, inb, elb, ob_ref))
    chains = [(d, h) for d in range(2) for h in range(DN_H)]
    S = {ch: s_sc[ch[0], ch[1]] for ch in chains}
    for j in range(g):
        r1, vnb = {}, {}
        for d, h in chains:
            w_r, u_r, qd_r = views[d][0], views[d][1], views[d][2]
            rs = slice((g - 1 - j if d else j) * CHUNK, (g - j if d else j + 1) * CHUNK)
            sl = slice(h * DN_DK, (h + 1) * DN_DK)
            r1[d, h] = _dot(jnp.concatenate([w_r[0, rs, sl], qd_r[0, rs, sl]], axis=0), S[d, h].astype(bf16))
        gla_outputs(j)
        for d, h in chains:
            u_r = views[d][1]
            rs = slice((g - 1 - j if d else j) * CHUNK, (g - j if d else j + 1) * CHUNK)
            sl = slice(h * DN_DK, (h + 1) * DN_DK)
            vnb[d, h] = (u_r[0, rs, sl] - r1[d, h][:CHUNK]).astype(bf16)
        for d, h in chains:
            kd_r, in_r, el_r, o_r = views[d][3], views[d][4], views[d][5], views[d][6]
            c = g - 1 - j if d else j
            rs = slice(c * CHUNK, (c + 1) * CHUNK)
            sl = slice(h * DN_DK, (h + 1) * DN_DK)
            sc = slice(h * CHUNK, (h + 1) * CHUNK)
            o_r[rs, sl] = (r1[d, h][CHUNK:] + _dot(in_r[0, rs, sc], vnb[d, h])).astype(bf16)
            S[d, h] = S[d, h] * el_r[0, c][:, sl] + _dot_tn(kd_r[0, rs, sl], vnb[d, h])
    for d, h in chains:
        s_sc[d, h] = S[d, h]


def _group_maps(lay, g):
    B = lay.B
    ngc, ngl = lay.Tc // (CHUNK * g), lay.T // (CHUNK * g)

    def fwd(b, n):
        return jnp.where(n < ngc, b * ngc + n, B * ngc + b * ngl + (n - ngc))

    def bwd(b, n):
        return jnp.where(n < ngc, b * ngc + (ngc - 1 - n), B * ngc + b * ngl + (ngl - 1 - (n - ngc)))

    return fwd, bwd, ngc + ngl


def _scan_group(lay):
    g = min(4, lay.Tc // CHUNK)
    assert (lay.Tc // CHUNK) % g == 0 and (lay.T // CHUNK) % g == 0
    return g


def _rec_scan(delta_in, gla_in, lay):
    w, u, qd, kd, intra, el = delta_in
    qi, kv, al = gla_in
    R = w.shape[1]
    g = _scan_group(lay)
    fwd, bwd, nstep = _group_maps(lay, g)
    wq = DN_H * DN_DK
    wv = DN_H * DN_DV
    wi = DN_H * CHUNK
    wk = GLA_H * GLA_DK
    rows = g * CHUNK

    def delta_view(m, d):
        sp = lambda width: pl.BlockSpec((1, rows, width), lambda b, n: (d, m(b, n), 0))
        return [sp(wq), sp(wv), sp(wq), sp(wq), sp(wi), pl.BlockSpec((1, g, 1, wq), lambda b, n: (d, m(b, n), 0, 0))]

    def gla_view(m, d):
        return [pl.BlockSpec((1, rows, wk), lambda b, n: (d, m(b, n), 0)),
                pl.BlockSpec((1, g, GLA_DV, wk), lambda b, n: (d, m(b, n), 0, 0)),
                pl.BlockSpec((1, g, 1, wk), lambda b, n: (d, m(b, n), 0, 0))]

    def out(m, width):
        return pl.BlockSpec((rows, width), lambda b, n: (m(b, n), 0))

    o_shape = jax.ShapeDtypeStruct((R, wv), bf16)
    go_shape = jax.ShapeDtypeStruct((R, GLA_H * GLA_DV), bf16)
    body = functools.partial(_rec_scan_body, g=g)
    return pl.pallas_call(
        body,
        out_shape=(o_shape, o_shape, go_shape, go_shape),
        grid=(lay.B, nstep),
        in_specs=delta_view(fwd, 0) + delta_view(bwd, 1) + gla_view(fwd, 0) + gla_view(bwd, 1),
        out_specs=(out(fwd, wv), out(bwd, wv), out(fwd, GLA_H * GLA_DV), out(bwd, GLA_H * GLA_DV)),
        scratch_shapes=[pltpu.VMEM((2, DN_H, DN_DK, DN_DV), f32), pltpu.VMEM((2, GLA_DV, wk), f32)],
        compiler_params=_cparams("parallel", "arbitrary"),
        name="rec_scan",
    )(w, u, qd, kd, intra, el, w, u, qd, kd, intra, el, qi, kv, al, qi, kv, al)


def _gla_prep_body(q_ref, k_ref, v_ref, la_ref, oi_ref, qi_ref, kv_ref, al_ref, *, nc):
    H = GLA_H
    insts = [(c, d) for c in range(nc) for d in range(2)]
    rows = [slice(c * CHUNK, (c + 1) * CHUNK) for c in range(nc)]
    incl4 = [_tri_masks(d == 1, H * CHUNK)[0] for d in range(2)]
    mbs = [_tri_masks(d == 1, CHUNK)[0].astype(bf16) for d in range(2)]
    b = {}
    for c, d in insts:
        hi, lo = _split2(la_ref[d, rows[c], :])
        b[c, d] = _dot(jnp.concatenate([mbs[d], mbs[d]], axis=1), jnp.concatenate([hi, lo], axis=0))
    att = {}
    for c, d in insts:
        q = q_ref[rows[c], :].astype(f32) * (GLA_DK ** -0.5)
        mid = CHUNK - 1 - CHUNK // 2 if d else CHUNK // 2
        b_mid = b[c, d][mid:mid + 1, :]
        qe = (q * jnp.exp(b[c, d] - b_mid)).astype(bf16)
        ke = (k_ref[rows[c], :] * jnp.exp(b_mid - b[c, d])).astype(bf16)
        att[c, d] = jnp.where(incl4[d], _dot_nt(qe, _block_diag(ke, H)), 0.0).astype(bf16)
        qi_ref[d, rows[c], :] = (q * jnp.exp(b[c, d])).astype(bf16)
    for c, d in insts:
        v_bf = v_ref[rows[c], :]
        oi_ref[d, rows[c], :] = _dot(att[c, d], _block_diag(v_bf, H)).astype(bf16)
        last = 0 if d else CHUNK - 1
        b_last = b[c, d][last:last + 1, :]
        ks = (k_ref[rows[c], :] * jnp.exp(b_last - b[c, d])).astype(bf16)
        kv = [_dot_tn(v_bf[:, h * GLA_DV:(h + 1) * GLA_DV], ks[:, h * GLA_DK:(h + 1) * GLA_DK]) for h in range(H)]
        kv_ref[d, c] = jnp.concatenate(kv, axis=1)
        al_ref[d, c] = jnp.exp(b_last)


def _gla_prep(ym, la, lay):
    R = ym.shape[0]
    tm = lay.tm
    nc = tm // CHUNK
    wk = GLA_H * GLA_DK
    wv = GLA_H * GLA_DV
    cq, ck, cv = OFF_GQ // wk, OFF_GK // wk, OFF_GV // wv
    drow = lambda i: (0, i, 0)
    body = functools.partial(_gla_prep_body, nc=nc)
    return pl.pallas_call(
        body,
        out_shape=(jax.ShapeDtypeStruct((2, R, wv), bf16), jax.ShapeDtypeStruct((2, R, wk), bf16),
                   jax.ShapeDtypeStruct((2, R // CHUNK, GLA_DV, wk), f32),
                   jax.ShapeDtypeStruct((2, R // CHUNK, 1, wk), f32)),
        grid=(R // tm,),
        in_specs=[pl.BlockSpec((tm, wk), lambda i: (i, cq)), pl.BlockSpec((tm, wk), lambda i: (i, ck)),
                  pl.BlockSpec((tm, wv), lambda i: (i, cv)), pl.BlockSpec((2, tm, wk), drow)],
        out_specs=(pl.BlockSpec((2, tm, wv), drow), pl.BlockSpec((2, tm, wk), drow),
                   pl.BlockSpec((2, nc, GLA_DV, wk), lambda i: (0, i, 0, 0)),
                   pl.BlockSpec((2, nc, 1, wk), lambda i: (0, i, 0, 0))),
        compiler_params=_cparams("parallel"),
        name="gla_prep",
    )(ym, ym, ym, la)


def _recout_body(df_ref, db_ref, gf_ref, gb_ref, gi_ref, z_ref, r_ref, x_ref, m_ref, dn_ref, gn_ref, w_ref, o_ref):
    up = lambda r: r.astype(f32)
    dn = up(df_ref[...]) + up(db_ref[...])
    gl = (up(gf_ref[...]) + up(gi_ref[0])) + (up(gb_ref[...]) + up(gi_ref[1]))
    z = up(z_ref[...])
    r = up(r_ref[...])
    parts = []
    for src, gate, gain, hd, nh in ((dn, z, dn_ref[...], DN_DV, DN_H), (gl, r, gn_ref[...], GLA_DV, GLA_H)):
        for h in range(nh):
            a = src[:, h * hd:(h + 1) * hd]
            a = a * lax.rsqrt(jnp.mean(a * a, axis=-1, keepdims=True) + EPS) * gain
            parts.append(a * _silu(gate[:, h * hd:(h + 1) * hd]))
    mix = jnp.concatenate(parts, axis=1).astype(bf16)
    o_ref[...] = x_ref[...] + m_ref[0][2:3, :] * _dot(mix, w_ref[...])


def _recout(dn_f, dn_b, gl_f, gl_b, gl_i, ym, xs, mods_l, dn_norm, gla_norm, w, lay):
    R, D = xs.shape
    tm = lay.tmf
    wd = DN_H * DN_DV
    wg = GLA_H * GLA_DV
    row = lambda i: (i, 0)
    return pl.pallas_call(
        _recout_body,
        out_shape=jax.ShapeDtypeStruct((R, D), f32),
        grid=(R // tm,),
        in_specs=[pl.BlockSpec((tm, wd), row), pl.BlockSpec((tm, wd), row),
                  pl.BlockSpec((tm, wg), row), pl.BlockSpec((tm, wg), row),
                  pl.BlockSpec((2, tm, wg), lambda i: (0, i, 0)),
                  pl.BlockSpec((tm, wd), lambda i: (i, OFF_DZ // wd)),
                  pl.BlockSpec((tm, wg), lambda i: (i, OFF_GR // wg)),
                  pl.BlockSpec((tm, D), row),
                  pl.BlockSpec((1, 6, D), lambda i: (lay.mod_row(i * tm), 0, 0)),
                  _const_spec(dn_norm.shape), _const_spec(gla_norm.shape), _const_spec(w.shape)],
        out_specs=pl.BlockSpec((tm, D), row),
        compiler_params=_cparams("parallel"),
        name="rec_outproj",
    )(dn_f, dn_b, gl_f, gl_b, gl_i, ym, ym, xs, mods_l, dn_norm, gla_norm, w)


def _rec_weights(w_in):
    sizes = (DN_H * DN_DK, DN_H * DN_DK, DN_H * DN_DV, DN_H * DN_DV, 2 * DN_H, 2 * DN_H,
             GLA_H * GLA_DK, GLA_H * GLA_DK, GLA_H * GLA_DV, GLA_H * GLA_DV, 2 * GLA_RANK)
    offs = [0]
    for s in sizes:
        offs.append(offs[-1] + s)
    seg = [w_in[:, offs[i]:offs[i + 1]] for i in range(len(sizes))]
    dq, dk, dv, dz, da, db, gq, gk, gv, gr, gg = seg
    main = jnp.concatenate([dq, dk, dv, dz, gq, gk, gv, gr], axis=1).astype(bf16)
    small = jnp.concatenate([da, db, gg], axis=1)
    small = jnp.pad(small, ((0, 0), (0, LANES - small.shape[1]))).astype(bf16)
    return main, small


def _pad_lanes(v, off=0):
    v = v.reshape(1, -1)
    return jnp.pad(v, ((0, 0), (off, LANES - off - v.shape[1])))


def _gla_gate_weights(w2):
    out = jnp.zeros((2, LANES, w2.shape[2]), f32)
    for d in range(2):
        out = out.at[d, SM_GG + d * GLA_RANK:SM_GG + (d + 1) * GLA_RANK, :].set(w2[d])
    hi = out.astype(bf16)
    lo = (out - hi.astype(f32)).astype(bf16)
    return hi, lo


def _rope_tables(lay):
    T, Tc, B = lay.T, lay.Tc, lay.B
    t = jnp.arange(T, dtype=jnp.int32)
    row = (t // GRID_W).astype(f32)
    col = (t % GRID_W).astype(f32)
    inv_freq = ROPE_THETA ** (-jnp.arange(ROPE_PAIRS, dtype=f32) / ROPE_PAIRS)
    ar = row[:, None] * inv_freq
    ac = col[:, None] * inv_freq
    cos = jnp.concatenate([jnp.cos(ar), jnp.cos(ar), jnp.cos(ac), jnp.cos(ac)], axis=1)
    sin = jnp.concatenate([-jnp.sin(ar), jnp.sin(ar), -jnp.sin(ac), jnp.sin(ac)], axis=1)
    cos = jnp.concatenate([jnp.ones((B * Tc, ATT_HD), f32)] + [cos] * B, axis=0)
    sin = jnp.concatenate([jnp.zeros((B * Tc, ATT_HD), f32)] + [sin] * B, axis=0)
    return cos, sin


def kernel(x, c, ctx, c_ctx, mod_w, mod_b, rec_w_in, rec_conv, dn_a_log, dn_dt_bias, dn_norm, gla_w2, gla_b2,
           gla_norm, rec_w_out, att_w_qkv, att_q_norm, att_k_norm, att_w_out, ffn_w_up, ffn_conv, ffn_w_down,
           final_norm):
    B, T, D = x.shape
    Tc = ctx.shape[1]
    depth = mod_w.shape[0]
    lay = _Layout(B, T, Tc)
    assert B + 1 <= MOD_ROWS
    xs = jnp.concatenate([ctx.reshape(B * Tc, D), x.reshape(B * T, D)], axis=0)
    cond = jnp.concatenate([c, c_ctx[None], jnp.zeros((MOD_ROWS - B - 1, D), f32)], axis=0)
    mods = _mods(cond, mod_w, mod_b)
    cos, sin = _rope_tables(lay)

    for i in range(depth):
        ml = mods[i]
        if i % 2 == 0:
            e = i // 2
            wm, ws = _rec_weights(rec_w_in[e])
            ym, ys = _recin(xs, ml, wm, ws, lay)
            w2h, w2l = _gla_gate_weights(gla_w2[e])
            q, k, v, gates, la = _recprep(
                ym, ys, rec_conv[e], _pad_lanes(dn_a_log[e], SM_DA), _pad_lanes(dn_dt_bias[e], SM_DA),
                w2h, w2l, gla_b2[e].reshape(2, 1, -1), lay)
            gl_i, qi, kv, al = _gla_prep(ym, la, lay)
            dn_f, dn_b, gl_f, gl_b = _rec_scan(_delta_prep(q, k, v, gates, lay), (qi, kv, al), lay)
            xs = _recout(dn_f, dn_b, gl_f, gl_b, gl_i, ym, xs, ml, dn_norm[e].reshape(1, -1),
                         gla_norm[e].reshape(1, -1), rec_w_out[e].astype(bf16), lay)
        else:
            o = i // 2
            q, k, v = _qkv(xs, ml, att_w_qkv[o].astype(bf16), att_q_norm[o].reshape(1, -1),
                           att_k_norm[o].reshape(1, -1), cos, sin, lay)
            a = _flash(q, k, v, lay)
            xs = _outproj(a, xs, ml, att_w_out[o].astype(bf16), lay)
        last = i == depth - 1
        xs = _ffn(xs, ml, ffn_w_up[i].astype(bf16), ffn_conv[i], ffn_w_down[i].astype(bf16), lay,
                  final_gain=final_norm.reshape(1, -1) if last else None)

    return xs.reshape(B, T, D)
```

```python
import functools
import math

import jax
import jax.numpy as jnp
from jax import lax
from jax.experimental import pallas as pl
from jax.experimental.pallas import tpu as pltpu

f32 = jnp.float32
bf16 = jnp.bfloat16

EPS = 1e-6
GRID_W = 64
DN_H = 4
DN_DK = 128
DN_DV = 128
GLA_H = 4
GLA_DK = 64
GLA_DV = 128
GLA_RANK = 16
GLA_TAU = 16.0
CHUNK = 64
ATT_H = 8
ATT_KVH = 2
ATT_G = ATT_H // ATT_KVH
ATT_HD = 128
ROPE_THETA = 10000.0
ROPE_PAIRS = ATT_HD // 4
LOG2E = math.log2(math.e)
FFN_CHUNK = 256
MOD_ROWS = 8
LANES = 128
VMEM_LIMIT = 56 * 1024 * 1024


def _cparams(*sem):
    return pltpu.CompilerParams(dimension_semantics=sem, vmem_limit_bytes=VMEM_LIMIT)


def _const_spec(shape):
    nd = len(shape)
    return pl.BlockSpec(shape, lambda *_: (0,) * nd, pipeline_mode=pl.Buffered(1))


class _Layout:
    def __init__(self, B, T, Tc):
        self.B, self.T, self.Tc = B, T, Tc
        self.R = B * Tc + B * T
        self.tm = min(256, Tc)
        assert Tc % self.tm == 0 and T % self.tm == 0 and T % CHUNK == 0 and Tc % CHUNK == 0
        self.tmf = min(512, B * Tc)
        assert (B * Tc) % self.tmf == 0 and T % self.tmf == 0
        self.seq_starts = [b * Tc for b in range(B)] + [B * Tc + b * T for b in range(B)]

    def mod_row(self, row0):
        return jnp.where(row0 < self.B * self.Tc, self.B, (row0 - self.B * self.Tc) // self.T)

    def is_seq_start(self, row0):
        return jnp.where(row0 < self.B * self.Tc, row0 % self.Tc == 0, (row0 - self.B * self.Tc) % self.T == 0)

    def edge_masks(self, rows):
        first = rows == self.seq_starts[0]
        last = rows == self.R - 1
        for s in self.seq_starts[1:]:
            first = jnp.logical_or(first, rows == s)
            last = jnp.logical_or(last, rows == s - 1)
        return first, last


def _normmod(x, m, shift_i, scale_i):
    ms = jnp.mean(x * x, axis=-1, keepdims=True)
    return (x * lax.rsqrt(ms + EPS)) * (1.0 + m[scale_i:scale_i + 1, :]) + m[shift_i:shift_i + 1, :]


def _silu(x):
    return x * jax.nn.sigmoid(x)


def _split2(x):
    hi = x.astype(bf16)
    lo = (x - hi.astype(f32)).astype(bf16)
    return hi, lo


def _dot(a, b):
    return jnp.dot(a, b, preferred_element_type=f32)


def _dot_nt(a, b):
    return lax.dot_general(a, b, (((1,), (1,)), ((), ())), preferred_element_type=f32)


def _dot_tn(a, b):
    return lax.dot_general(a, b, (((0,), (0,)), ((), ())), preferred_element_type=f32)


def _mod_body(c_ref, w_ref, b_ref, o_ref):
    a = _silu(c_ref[...]).astype(bf16)
    o_ref[0] = _dot(a, w_ref[0].astype(bf16)) + b_ref[0]


def _mods(cond, mod_w, mod_b):
    depth, D, D6 = mod_w.shape
    tn = 1536
    out = pl.pallas_call(
        _mod_body,
        out_shape=jax.ShapeDtypeStruct((depth, MOD_ROWS, D6), f32),
        grid=(depth, D6 // tn),
        in_specs=[pl.BlockSpec((MOD_ROWS, D), lambda l, j: (0, 0)),
                  pl.BlockSpec((1, D, tn), lambda l, j: (l, 0, j)),
                  pl.BlockSpec((1, 1, tn), lambda l, j: (l, 0, j))],
        out_specs=pl.BlockSpec((1, MOD_ROWS, tn), lambda l, j: (l, 0, j)),
        compiler_params=_cparams("parallel", "parallel"),
        name="mods",
    )(cond, mod_w, mod_b.reshape(depth, 1, D6))
    return out.reshape(depth, MOD_ROWS, 6, D)


def _ffn_body(xp_ref, x_ref, xn_ref, m_ref, wu_ref, cw_ref, wd_ref, *rest, lay, halo, off, final):
    fg_ref, o_ref, h_sc, g_sc, acc_sc = rest if final else (None,) + rest
    tm = lay.tmf
    F = wd_ref.shape[0]
    row0 = (pl.program_id(0) + off) * tm
    m = m_ref[0]
    x = x_ref[...]
    xe = jnp.concatenate([xp_ref[...], x, xn_ref[...]], axis=0)
    h_sc[...] = _normmod(xe, m, 3, 4).astype(bf16)
    kill_prev, kill_next = lay.edge_masks(row0 + lax.broadcasted_iota(jnp.int32, (tm, 1), 0))
    acc_sc[...] = jnp.zeros_like(acc_sc)

    for c0 in range(0, F, FFN_CHUNK):
        cs = slice(c0, c0 + FFN_CHUNK)
        g_sc[...] = _dot(h_sc[...], wu_ref[:, cs])
        val = _dot(h_sc[pl.ds(halo, tm), :], wu_ref[:, F + c0:F + c0 + FFN_CHUNK])
        gm1 = jnp.where(kill_prev, 0.0, g_sc[pl.ds(halo - 1, tm), :])
        g0 = g_sc[pl.ds(halo, tm), :]
        gp1 = jnp.where(kill_next, 0.0, g_sc[pl.ds(halo + 1, tm), :])
        gate = gm1 * cw_ref[0:1, cs] + g0 * cw_ref[1:2, cs] + gp1 * cw_ref[2:3, cs]
        act = (_silu(gate) * val).astype(bf16)
        acc_sc[...] += _dot(act, wd_ref[cs, :])
    y = x + m[5:6, :] * acc_sc[...]
    if final:
        y = y * lax.rsqrt(jnp.mean(y * y, axis=-1, keepdims=True) + EPS) * fg_ref[...]
    o_ref[...] = y


def _ffn(xs, mods_l, wu, cw, wd, lay, final_gain=None):
    R, D = xs.shape
    tm = lay.tmf
    halo = 16
    cf = FFN_CHUNK
    assert wd.shape[0] % cf == 0 and wu.shape[1] == 2 * wd.shape[0]
    nh = tm // halo
    nhb = R // halo
    final = final_gain is not None
    off = lay.B * lay.Tc // tm if final else 0
    body = functools.partial(_ffn_body, lay=lay, halo=halo, off=off, final=final)
    extra_specs = [_const_spec(final_gain.shape)] if final else []
    extra_args = [final_gain] if final else []
    return pl.pallas_call(
        body,
        out_shape=jax.ShapeDtypeStruct((R - off * tm, D), f32),
        grid=(R // tm - off,),
        in_specs=[pl.BlockSpec((halo, D), lambda i: (jnp.maximum((i + off) * nh - 1, 0), 0)),
                  pl.BlockSpec((tm, D), lambda i: (i + off, 0)),
                  pl.BlockSpec((halo, D), lambda i: (jnp.minimum((i + off + 1) * nh, nhb - 1), 0)),
                  pl.BlockSpec((1, 6, D), lambda i: (lay.mod_row((i + off) * tm), 0, 0)),
                  _const_spec(wu.shape), _const_spec(cw.shape), _const_spec(wd.shape)]
        + extra_specs,
        out_specs=pl.BlockSpec((tm, D), lambda i: (i, 0)),
        scratch_shapes=[pltpu.VMEM((tm + 2 * halo, D), bf16),
                        pltpu.VMEM((tm + 2 * halo, cf), f32),
                        pltpu.VMEM((tm, D), f32)],
        compiler_params=_cparams("parallel"),
        name="conv_ffn",
    )(xs, xs, xs, mods_l, wu, cw, wd, *extra_args)


def _qkv_body(x_ref, m_ref, w_ref, qn_ref, kn_ref, cos_ref, sin_ref, q_ref, k_ref, v_ref, *, lay):
    h = _normmod(x_ref[...], m_ref[0], 0, 1).astype(bf16)
    y = _dot(h, w_ref[...])
    is_ctx = pl.program_id(0) * lay.tmf < lay.B * lay.Tc
    cos = jnp.where(is_ctx, 1.0, cos_ref[...])
    sin = jnp.where(is_ctx, 0.0, sin_ref[...])
    lane = lax.broadcasted_iota(jnp.int32, cos.shape, 1)
    first = (lane % (2 * ROPE_PAIRS)) < ROPE_PAIRS

    def norm_rope(z, gain):
        z = z * lax.rsqrt(jnp.mean(z * z, axis=-1, keepdims=True) + EPS) * gain
        partner = jnp.where(first, pltpu.roll(z, LANES - ROPE_PAIRS, axis=1), pltpu.roll(z, ROPE_PAIRS, axis=1))
        return z * cos + partner * sin

    nq = ATT_H * ATT_HD
    for hh in range(ATT_H):
        z = norm_rope(y[:, hh * ATT_HD:(hh + 1) * ATT_HD], qn_ref[...])
        q_ref[:, hh * ATT_HD:(hh + 1) * ATT_HD] = (z * (ATT_HD ** -0.5 * LOG2E)).astype(bf16)
    for kk in range(ATT_KVH):
        z = norm_rope(y[:, nq + kk * ATT_HD:nq + (kk + 1) * ATT_HD], kn_ref[...])
        k_ref[:, kk * ATT_HD:(kk + 1) * ATT_HD] = z.astype(bf16)
    nk = nq + ATT_KVH * ATT_HD
    ones = jnp.ones((y.shape[0], ATT_HD), bf16)
    for kk in range(ATT_KVH):
        v_ref[:, 2 * kk * ATT_HD:(2 * kk + 1) * ATT_HD] = y[:, nk + kk * ATT_HD:nk + (kk + 1) * ATT_HD].astype(bf16)
        v_ref[:, (2 * kk + 1) * ATT_HD:(2 * kk + 2) * ATT_HD] = ones


def _qkv(xs, mods_l, w, qn, kn, cos, sin, lay):
    R, D = xs.shape
    tm = lay.tmf
    nq = ATT_H * ATT_HD
    nkv = ATT_KVH * ATT_HD
    nct, nlt = lay.B * lay.Tc // tm, lay.T // tm
    pos = lambda i: (jnp.where(i < nct, 0, (i - nct) % nlt), 0)
    kv_rows = lambda i: (jnp.where(i < nct, lay.B * nlt + i, i - nct), 0)
    return pl.pallas_call(
        functools.partial(_qkv_body, lay=lay),
        out_shape=(jax.ShapeDtypeStruct((R, nq), bf16), jax.ShapeDtypeStruct((R, nkv), bf16),
                   jax.ShapeDtypeStruct((R, 2 * nkv), bf16)),
        grid=(R // tm,),
        in_specs=[pl.BlockSpec((tm, D), lambda i: (i, 0)),
                  pl.BlockSpec((1, 6, D), lambda i: (lay.mod_row(i * tm), 0, 0)),
                  _const_spec(w.shape), _const_spec(qn.shape), _const_spec(kn.shape),
                  pl.BlockSpec((tm, ATT_HD), pos),
                  pl.BlockSpec((tm, ATT_HD), pos)],
        out_specs=(pl.BlockSpec((tm, nq), lambda i: (i, 0)), pl.BlockSpec((tm, nkv), kv_rows),
                   pl.BlockSpec((tm, 2 * nkv), kv_rows)),
        compiler_params=_cparams("parallel"),
        name="qkv_rope",
    )(xs, mods_l, w, qn, kn, cos, sin)


def _flash_body(q_ref, kc_ref, vc_ref, kl_ref, vl_ref, o_ref, m_sc, acc_sc, *, tq, ncq, ck):
    qi = pl.program_id(2)
    q = q_ref[...]
    q4 = jnp.concatenate([q[:, g * ATT_HD:(g + 1) * ATT_HD] for g in range(ATT_G)], axis=0)
    m_sc[...] = jnp.full_like(m_sc, -jnp.inf)
    acc_sc[...] = jnp.zeros_like(acc_sc)

    def chunk(s, v):
        m_prev = m_sc[...]
        m_new = jnp.maximum(m_prev, jnp.max(s, axis=-1, keepdims=True))
        p = jnp.exp2(s - jnp.tile(m_new, (1, s.shape[1] // LANES)))
        alpha = jnp.exp2(m_prev - m_new)
        acc_sc[...] = jnp.tile(alpha, (1, 2)) * acc_sc[...] + _dot(p.astype(bf16), v)
        m_sc[...] = m_new

    chunk(_dot_nt(q4, kc_ref[...]), vc_ref[...])

    @pl.when(qi >= ncq)
    def _():
        n = kl_ref.shape[0] // ck
        s_next = _dot_nt(q4, kl_ref[0:ck, :])
        for c in range(n):
            s = s_next
            if c + 1 < n:
                s_next = _dot_nt(q4, kl_ref[(c + 1) * ck:(c + 2) * ck, :])
            chunk(s, vl_ref[c * ck:(c + 1) * ck, :])

    acc = acc_sc[...]
    out = acc[:, :ATT_HD] / acc[:, ATT_HD:]
    for g in range(ATT_G):
        o_ref[:, g * ATT_HD:(g + 1) * ATT_HD] = out[g * tq:(g + 1) * tq, :].astype(bf16)


def _flash(q, k, v1, lay):
    R = q.shape[0]
    B, T, Tc = lay.B, lay.T, lay.Tc
    tq = lay.tm
    ncq, nlq = Tc // tq, T // tq
    gw = ATT_G * ATT_HD
    ck = min(2048, T)
    assert T % ck == 0
    assert (B * T) % Tc == 0

    def q_map(b, kv, qi):
        return (jnp.where(qi < ncq, b * ncq + qi, B * ncq + b * nlq + (qi - ncq)), kv)

    lat_map = lambda b, kv, qi: (b, kv)
    ctx_map = lambda b, kv, qi: (B * T // Tc + b, kv)
    body = functools.partial(_flash_body, tq=tq, ncq=ncq, ck=ck)
    return pl.pallas_call(
        body,
        out_shape=jax.ShapeDtypeStruct((R, ATT_H * ATT_HD), bf16),
        grid=(B, ATT_KVH, ncq + nlq),
        in_specs=[pl.BlockSpec((tq, gw), q_map),
                  pl.BlockSpec((Tc, ATT_HD), ctx_map), pl.BlockSpec((Tc, 2 * ATT_HD), ctx_map),
                  pl.BlockSpec((T, ATT_HD), lat_map), pl.BlockSpec((T, 2 * ATT_HD), lat_map)],
        out_specs=pl.BlockSpec((tq, gw), q_map),
        scratch_shapes=[pltpu.VMEM((ATT_G * tq, LANES), f32), pltpu.VMEM((ATT_G * tq, 2 * ATT_HD), f32)],
        compiler_params=_cparams("parallel", "parallel", "parallel"),
        name="flash_gqa",
    )(q, k, v1, k, v1)


def _outproj_body(a_ref, x_ref, m_ref, w_ref, o_ref):
    o_ref[...] = x_ref[...] + m_ref[0][2:3, :] * _dot(a_ref[...], w_ref[...])


def _outproj(a, xs, mods_l, w, lay):
    R, D = xs.shape
    tm = lay.tmf
    K = a.shape[1]
    return pl.pallas_call(
        _outproj_body,
        out_shape=jax.ShapeDtypeStruct((R, D), f32),
        grid=(R // tm,),
        in_specs=[pl.BlockSpec((tm, K), lambda i: (i, 0)),
                  pl.BlockSpec((tm, D), lambda i: (i, 0)),
                  pl.BlockSpec((1, 6, D), lambda i: (lay.mod_row(i * tm), 0, 0)),
                  _const_spec(w.shape)],
        out_specs=pl.BlockSpec((tm, D), lambda i: (i, 0)),
        compiler_params=_cparams("parallel"),
        name="att_outproj",
    )(a, xs, mods_l, w)


REC_MAIN = 2 * DN_H * DN_DK + 2 * DN_H * DN_DV + 2 * GLA_H * GLA_DK + 2 * GLA_H * GLA_DV
REC_QKV = 2 * DN_H * DN_DK + DN_H * DN_DV
OFF_DZ = REC_QKV
OFF_GQ = OFF_DZ + DN_H * DN_DV
OFF_GK = OFF_GQ + GLA_H * GLA_DK
OFF_GV = OFF_GK + GLA_H * GLA_DK
OFF_GR = OFF_GV + GLA_H * GLA_DV
SM_DA = 0
SM_DB = 2 * DN_H
SM_GG = 4 * DN_H


def _recin_body(x_ref, m_ref, wm_ref, ws_ref, ym_ref, ys_ref):
    h = _normmod(x_ref[...], m_ref[0], 0, 1).astype(bf16)
    ym_ref[...] = _dot(h, wm_ref[...]).astype(bf16)
    ys_ref[...] = _dot(h, ws_ref[...])


def _recin(xs, mods_l, wm, ws, lay):
    R, D = xs.shape
    tm = lay.tmf
    return pl.pallas_call(
        _recin_body,
        out_shape=(jax.ShapeDtypeStruct((R, REC_MAIN), bf16), jax.ShapeDtypeStruct((R, LANES), f32)),
        grid=(R // tm,),
        in_specs=[pl.BlockSpec((tm, D), lambda i: (i, 0)),
                  pl.BlockSpec((1, 6, D), lambda i: (lay.mod_row(i * tm), 0, 0)),
                  _const_spec(wm.shape), _const_spec(ws.shape)],
        out_specs=(pl.BlockSpec((tm, REC_MAIN), lambda i: (i, 0)), pl.BlockSpec((tm, LANES), lambda i: (i, 0))),
        compiler_params=_cparams("parallel"),
        name="rec_inproj",
    )(xs, mods_l, wm, ws)


def _recprep_body(yp_ref, y_ref, yn_ref, ys_ref, cw_ref, av_ref, dtb_ref, w2h_ref, w2l_ref, b2_ref,
                  q_ref, k_ref, v_ref, g_ref, la_ref, e_sc, *, lay, halo):
    tm = lay.tm
    row0 = pl.program_id(0) * tm
    e_sc[0:halo, :] = yp_ref[...].astype(f32)
    e_sc[halo:halo + tm, :] = y_ref[...].astype(f32)
    e_sc[halo + tm:halo + tm + halo, :] = yn_ref[...].astype(f32)
    rowi = lax.broadcasted_iota(jnp.int32, (tm, 1), 0)
    kill_prev = jnp.logical_and(rowi == 0, lay.is_seq_start(row0))
    kill_next = jnp.logical_and(rowi == tm - 1, jnp.logical_or(lay.is_seq_start(row0 + tm), row0 + tm == lay.R))
    cw = cw_ref[...]
    gm1 = jnp.where(kill_prev, 0.0, e_sc[pl.ds(halo - 1, tm), :])
    g0 = e_sc[pl.ds(halo, tm), :]
    gp1 = jnp.where(kill_next, 0.0, e_sc[pl.ds(halo + 1, tm), :])
    s = _silu(gm1 * cw[0:1, :] + g0 * cw[1:2, :] + gp1 * cw[2:3, :])
    nqk = DN_H * DN_DK
    for h in range(DN_H):
        z = s[:, h * DN_DK:(h + 1) * DN_DK]
        qn = z * lax.rsqrt(jnp.sum(z * z, axis=-1, keepdims=True) + EPS) * (DN_DK ** -0.5)
        q_ref[:, h * DN_DK:(h + 1) * DN_DK] = qn.astype(bf16)
        z = s[:, nqk + h * DN_DK:nqk + (h + 1) * DN_DK]
        k_ref[:, h * DN_DK:(h + 1) * DN_DK] = (z * lax.rsqrt(jnp.sum(z * z, axis=-1, keepdims=True) + EPS)).astype(bf16)
    v_ref[...] = s[:, 2 * nqk:].astype(bf16)
    z = ys_ref[...]
    lane = lax.broadcasted_iota(jnp.int32, z.shape, 1)
    g = -jnp.exp(av_ref[...]) * jax.nn.softplus(z + dtb_ref[...])
    beta = jax.nn.sigmoid(z)
    g_ref[...] = jnp.where(lane < SM_DB, g, jnp.where(lane < SM_GG, beta, 0.0))
    zh, zl = _split2(z)
    for d in range(2):
        la = _dot(zh, w2h_ref[d]) + _dot(zl, w2h_ref[d]) + _dot(zh, w2l_ref[d]) + b2_ref[d]
        la_ref[d] = jax.nn.log_sigmoid(la) / GLA_TAU


def _recprep(ym, ys, cw, av, dtb, w2h, w2l, b2, lay):
    R = ym.shape[0]
    tm = lay.tm
    halo = 16
    nh = tm // halo
    nhb = R // halo
    nd = DN_H * DN_DK
    body = functools.partial(_recprep_body, lay=lay, halo=halo)
    return pl.pallas_call(
        body,
        out_shape=(jax.ShapeDtypeStruct((R, nd), bf16), jax.ShapeDtypeStruct((R, nd), bf16),
                   jax.ShapeDtypeStruct((R, DN_H * DN_DV), bf16), jax.ShapeDtypeStruct((R, LANES), f32),
                   jax.ShapeDtypeStruct((2, R, GLA_H * GLA_DK), f32)),
        grid=(R // tm,),
        in_specs=[pl.BlockSpec((halo, REC_QKV), lambda i: (jnp.maximum(i * nh - 1, 0), 0)),
                  pl.BlockSpec((tm, REC_QKV), lambda i: (i, 0)),
                  pl.BlockSpec((halo, REC_QKV), lambda i: (jnp.minimum((i + 1) * nh, nhb - 1), 0)),
                  pl.BlockSpec((tm, LANES), lambda i: (i, 0)),
                  _const_spec(cw.shape), _const_spec(av.shape), _const_spec(dtb.shape),
                  _const_spec(w2h.shape), _const_spec(w2l.shape), _const_spec(b2.shape)],
        out_specs=(pl.BlockSpec((tm, nd), lambda i: (i, 0)), pl.BlockSpec((tm, nd), lambda i: (i, 0)),
                   pl.BlockSpec((tm, DN_H * DN_DV), lambda i: (i, 0)), pl.BlockSpec((tm, LANES), lambda i: (i, 0)),
                   pl.BlockSpec((2, tm, GLA_H * GLA_DK), lambda i: (0, i, 0))),
        scratch_shapes=[pltpu.VMEM((tm + 2 * halo, REC_QKV), f32)],
        compiler_params=_cparams("parallel"),
        name="rec_prep",
    )(ym, ym, ym, ys, cw, av, dtb, w2h, w2l, b2)


def _tri_masks(rev, width):
    ri = lax.broadcasted_iota(jnp.int32, (CHUNK, width), 0)
    ci = lax.broadcasted_iota(jnp.int32, (CHUNK, width), 1) % CHUNK
    if rev:
        return ri <= ci, ri < ci, ri >= ci
    return ri >= ci, ri > ci, ri <= ci


def _block_diag(x, nblk):
    w = x.shape[1]
    t = jnp.concatenate([x] * nblk, axis=0)
    rb = lax.broadcasted_iota(jnp.int32, t.shape, 0) // CHUNK
    cb = lax.broadcasted_iota(jnp.int32, t.shape, 1) // (w // nblk)
    return jnp.where(rb == cb, t, jnp.zeros_like(t))


def _gate_cols(gt, d, width):
    H = DN_H
    g = [jnp.broadcast_to(gt[:, SM_DA + d * H + h:SM_DA + d * H + h + 1], (CHUNK, width)) for h in range(H)]
    b = [jnp.broadcast_to(gt[:, SM_DB + d * H + h:SM_DB + d * H + h + 1], (CHUNK, width)) for h in range(H)]
    return jnp.concatenate(g, axis=1), jnp.concatenate(b, axis=1)


def _delta_prep_body(q_ref, k_ref, v_ref, g_ref, w_ref, u_ref, qd_ref, kd_ref, in_ref, el_ref, *, nc):
    H = DN_H
    insts = [(c, d) for c in range(nc) for d in range(2)]
    rows = [slice(c * CHUNK, (c + 1) * CHUNK) for c in range(nc)]
    masks = [_tri_masks(d == 1, H * CHUNK) for d in range(2)]
    mbs = [_tri_masks(d == 1, CHUNK)[0].astype(bf16) for d in range(2)]
    neg1 = jnp.full((CHUNK, CHUNK), -1.0, bf16)
    ri = lax.broadcasted_iota(jnp.int32, (CHUNK, H * CHUNK), 0)
    ci = lax.broadcasted_iota(jnp.int32, (CHUNK, H * CHUNK), 1) % CHUNK

    kkqk = []
    for c in range(nc):
        k_bf = k_ref[rows[c], :]
        kq = jnp.concatenate([k_bf, q_ref[rows[c], :]], axis=0)
        kkqk.append(_dot_nt(kq, _block_diag(k_bf, H)))

    G, diff = {}, {}
    for c, d in insts:
        gt = g_ref[rows[c], :]
        incl_t4 = masks[d][2]
        g128, _ = _gate_cols(gt, d, DN_DK)
        g64, _ = _gate_cols(gt, d, CHUNK)
        hi, lo = _split2(g128)
        G[c, d] = _dot(jnp.concatenate([mbs[d], mbs[d]], axis=1), jnp.concatenate([hi, lo], axis=0))
        hi6, lo6 = _split2(g64)
        zero = jnp.zeros_like(hi6)
        rhs = jnp.concatenate([hi6, lo6, jnp.where(incl_t4, hi6, zero), jnp.where(incl_t4, lo6, zero)], axis=0)
        diff[c, d] = _dot(jnp.concatenate([mbs[d], mbs[d], neg1, neg1], axis=1), rhs)

    a, p = {}, {}
    for c, d in insts:
        incl4, strict4, _ = masks[d]
        decay = jnp.where(incl4, jnp.exp(diff[c, d]), 0.0)
        _, b64 = _gate_cols(g_ref[rows[c], :], d, CHUNK)
        a[c, d] = jnp.where(strict4, b64 * kkqk[c][:CHUNK] * decay, 0.0)
        in_ref[d, rows[c], :] = (kkqk[c][CHUNK:] * decay).astype(bf16)
        p[c, d] = jnp.where(ri == ci, 1.0, 0.0) - jnp.where(ri // 2 == ci // 2, a[c, d], 0.0)

    m = 2
    while m < CHUNK:
        sib = jnp.logical_and(ri // (2 * m) == ci // (2 * m), ri // m != ci // m)
        x = {}
        for i in insts:
            x[i] = _dot(p[i].astype(bf16), _block_diag(jnp.where(sib, a[i], 0.0).astype(bf16), H))
        for i in insts:
            p[i] = p[i] - _dot(x[i].astype(bf16), _block_diag(p[i].astype(bf16), H))
        m *= 2

    for c, d in insts:
        rs = rows[c]
        Gc = G[c, d]
        tb = p[c, d].astype(bf16)
        last = 0 if d else CHUNK - 1
        g_last = Gc[last:last + 1, :]
        e_g = jnp.exp(Gc)
        _, b128 = _gate_cols(g_ref[rs, :], d, DN_DK)
        k_all = k_ref[rs, :]
        vbeta = v_ref[rs, :] * b128
        kbg = k_all * b128 * e_g
        qd_ref[d, rs, :] = (q_ref[rs, :] * e_g).astype(bf16)
        kd_ref[d, rs, :] = (k_all * jnp.exp(g_last - Gc)).astype(bf16)
        el_ref[d, c] = jnp.exp(g_last)
        for h in range(H):
            sl = slice(h * DN_DK, (h + 1) * DN_DK)
            sc = slice(h * CHUNK, (h + 1) * CHUNK)
            uw = _dot(tb[:, sc], jnp.concatenate([vbeta[:, sl], kbg[:, sl]], axis=1).astype(bf16))
            u_ref[d, rs, sl] = uw[:, :DN_DV]
            w_ref[d, rs, sl] = uw[:, DN_DV:].astype(bf16)


def _delta_prep(q, k, v, gates, lay):
    R = q.shape[0]
    tm = lay.tm
    nc = tm // CHUNK
    wq = DN_H * DN_DK
    wv = DN_H * DN_DV
    wi = DN_H * CHUNK
    row = lambda i: (i, 0)
    drow = lambda i: (0, i, 0)
    body = functools.partial(_delta_prep_body, nc=nc)
    return pl.pallas_call(
        body,
        out_shape=(jax.ShapeDtypeStruct((2, R, wq), bf16), jax.ShapeDtypeStruct((2, R, wv), f32),
                   jax.ShapeDtypeStruct((2, R, wq), bf16), jax.ShapeDtypeStruct((2, R, wq), bf16),
                   jax.ShapeDtypeStruct((2, R, wi), bf16), jax.ShapeDtypeStruct((2, R // CHUNK, 1, wq), f32)),
        grid=(R // tm,),
        in_specs=[pl.BlockSpec((tm, wq), row), pl.BlockSpec((tm, wq), row), pl.BlockSpec((tm, wv), row),
                  pl.BlockSpec((tm, LANES), row)],
        out_specs=(pl.BlockSpec((2, tm, wq), drow), pl.BlockSpec((2, tm, wv), drow),
                   pl.BlockSpec((2, tm, wq), drow), pl.BlockSpec((2, tm, wq), drow),
                   pl.BlockSpec((2, tm, wi), drow), pl.BlockSpec((2, nc, 1, wq), lambda i: (0, i, 0, 0))),
        compiler_params=_cparams("parallel"),
        name="delta_prep",
    )(q, k, v, gates)


def _rec_scan_body(wf, uf, qdf, kdf, inf, elf, wb, ub, qdb, kdb, inb, elb, qif, kvf, alf, qib, kvb, alb,
                   of_ref, ob_ref, gof_ref, gob_ref, s_sc, st_sc, *, g):
    @pl.when(pl.program_id(1) == 0)
    def _():
        s_sc[...] = jnp.zeros_like(s_sc)
        st_sc[...] = jnp.zeros_like(st_sc)

    gviews = ((qif, kvf, alf, gof_ref), (qib, kvb, alb, gob_ref))
    st = [st_sc[0], st_sc[1]]
    gstate = {}
    for j in range(g):
        for d in range(2):
            c = g - 1 - j if d else j
            gstate[d, c] = st[d].astype(bf16)
            st[d] = st[d] * gviews[d][2][0, c] + gviews[d][1][0, c]
    st_sc[0] = st[0]
    st_sc[1] = st[1]

    def gla_outputs(c):
        for d in range(2):
            qi_r, o_r = gviews[d][0], gviews[d][3]
            rs = slice(c * CHUNK, (c + 1) * CHUNK)
            for h in range(GLA_H):
                sk = slice(h * GLA_DK, (h + 1) * GLA_DK)
                o_r[rs, h * GLA_DV:(h + 1) * GLA_DV] = _dot_nt(qi_r[0, rs, sk], gstate[d, c][:, sk]).astype(bf16)

    views = ((wf, uf, qdf, kdf, inf, elf, of_ref), (wb, ub, qdb, kdb, inb, elb, ob_ref))
    chains = [(d, h) for d in range(2) for h in range(DN_H)]
    S = {ch: s_sc[ch[0], ch[1]] for ch in chains}
    for j in range(g):
        r1, vnb = {}, {}
        for d, h in chains:
            w_r, u_r, qd_r = views[d][0], views[d][1], views[d][2]
            rs = slice((g - 1 - j if d else j) * CHUNK, (g - j if d else j + 1) * CHUNK)
            sl = slice(h * DN_DK, (h + 1) * DN_DK)
            r1[d, h] = _dot(jnp.concatenate([w_r[0, rs, sl], qd_r[0, rs, sl]], axis=0), S[d, h].astype(bf16))
        gla_outputs(j)
        for d, h in chains:
            u_r = views[d][1]
            rs = slice((g - 1 - j if d else j) * CHUNK, (g - j if d else j + 1) * CHUNK)
            sl = slice(h * DN_DK, (h + 1) * DN_DK)
            vnb[d, h] = (u_r[0, rs, sl] - r1[d, h][:CHUNK]).astype(bf16)
        for d, h in chains:
            kd_r, in_r, el_r, o_r = views[d][3], views[d][4], views[d][5], views[d][6]
            c = g - 1 - j if d else j
            rs = slice(c * CHUNK, (c + 1) * CHUNK)
            sl = slice(h * DN_DK, (h + 1) * DN_DK)
            sc = slice(h * CHUNK, (h + 1) * CHUNK)
            o_r[rs, sl] = (r1[d, h][CHUNK:] + _dot(in_r[0, rs, sc], vnb[d, h])).astype(bf16)
            S[d, h] = S[d, h] * el_r[0, c][:, sl] + _dot_tn(kd_r[0, rs, sl], vnb[d, h])
    for d, h in chains:
        s_sc[d, h] = S[d, h]


def _group_maps(lay, g):
    B = lay.B
    ngc, ngl = lay.Tc // (CHUNK * g), lay.T // (CHUNK * g)

    def fwd(b, n):
        return jnp.where(n < ngc, b * ngc + n, B * ngc + b * ngl + (n - ngc))

    def bwd(b, n):
        return jnp.where(n < ngc, b * ngc + (ngc - 1 - n), B * ngc + b * ngl + (ngl - 1 - (n - ngc)))

    return fwd, bwd, ngc + ngl


def _scan_group(lay):
    g = min(4, lay.Tc // CHUNK)
    assert (lay.Tc // CHUNK) % g == 0 and (lay.T // CHUNK) % g == 0
    return g


def _rec_scan(delta_in, gla_in, lay):
    w, u, qd, kd, intra, el = delta_in
    qi, kv, al = gla_in
    R = w.shape[1]
    g = _scan_group(lay)
    fwd, bwd, nstep = _group_maps(lay, g)
    wq = DN_H * DN_DK
    wv = DN_H * DN_DV
    wi = DN_H * CHUNK
    wk = GLA_H * GLA_DK
    rows = g * CHUNK

    def delta_view(m, d):
        sp = lambda width: pl.BlockSpec((1, rows, width), lambda b, n: (d, m(b, n), 0))
        return [sp(wq), sp(wv), sp(wq), sp(wq), sp(wi), pl.BlockSpec((1, g, 1, wq), lambda b, n: (d, m(b, n), 0, 0))]

    def gla_view(m, d):
        return [pl.BlockSpec((1, rows, wk), lambda b, n: (d, m(b, n), 0)),
                pl.BlockSpec((1, g, GLA_DV, wk), lambda b, n: (d, m(b, n), 0, 0)),
                pl.BlockSpec((1, g, 1, wk), lambda b, n: (d, m(b, n), 0, 0))]

    def out(m, width):
        return pl.BlockSpec((rows, width), lambda b, n: (m(b, n), 0))

    o_shape = jax.ShapeDtypeStruct((R, wv), bf16)
    go_shape = jax.ShapeDtypeStruct((R, GLA_H * GLA_DV), bf16)
    body = functools.partial(_rec_scan_body, g=g)
    return pl.pallas_call(
        body,
        out_shape=(o_shape, o_shape, go_shape, go_shape),
        grid=(lay.B, nstep),
        in_specs=delta_view(fwd, 0) + delta_view(bwd, 1) + gla_view(fwd, 0) + gla_view(bwd, 1),
        out_specs=(out(fwd, wv), out(bwd, wv), out(fwd, GLA_H * GLA_DV), out(bwd, GLA_H * GLA_DV)),
        scratch_shapes=[pltpu.VMEM((2, DN_H, DN_DK, DN_DV), f32), pltpu.VMEM((2, GLA_DV, wk), f32)],
        compiler_params=_cparams("parallel", "arbitrary"),
        name="rec_scan",
    )(w, u, qd, kd, intra, el, w, u, qd, kd, intra, el, qi, kv, al, qi, kv, al)


def _gla_prep_body(q_ref, k_ref, v_ref, la_ref, oi_ref, qi_ref, kv_ref, al_ref, *, nc):
    H = GLA_H
    insts = [(c, d) for c in range(nc) for d in range(2)]
    rows = [slice(c * CHUNK, (c + 1) * CHUNK) for c in range(nc)]
    incl4 = [_tri_masks(d == 1, H * CHUNK)[0] for d in range(2)]
    mbs = [_tri_masks(d == 1, CHUNK)[0].astype(bf16) for d in range(2)]
    b = {}
    for c, d in insts:
        hi, lo = _split2(la_ref[d, rows[c], :])
        b[c, d] = _dot(jnp.concatenate([mbs[d], mbs[d]], axis=1), jnp.concatenate([hi, lo], axis=0))
    att = {}
    for c, d in insts:
        q = q_ref[rows[c], :].astype(f32) * (GLA_DK ** -0.5)
        mid = CHUNK - 1 - CHUNK // 2 if d else CHUNK // 2
        b_mid = b[c, d][mid:mid + 1, :]
        qe = (q * jnp.exp(b[c, d] - b_mid)).astype(bf16)
        ke = (k_ref[rows[c], :] * jnp.exp(b_mid - b[c, d])).astype(bf16)
        att[c, d] = jnp.where(incl4[d], _dot_nt(qe, _block_diag(ke, H)), 0.0).astype(bf16)
        qi_ref[d, rows[c], :] = (q * jnp.exp(b[c, d])).astype(bf16)
    for c, d in insts:
        v_bf = v_ref[rows[c], :]
        oi_ref[d, rows[c], :] = _dot(att[c, d], _block_diag(v_bf, H)).astype(bf16)
        last = 0 if d else CHUNK - 1
        b_last = b[c, d][last:last + 1, :]
        ks = (k_ref[rows[c], :] * jnp.exp(b_last - b[c, d])).astype(bf16)
        kv = [_dot_tn(v_bf[:, h * GLA_DV:(h + 1) * GLA_DV], ks[:, h * GLA_DK:(h + 1) * GLA_DK]) for h in range(H)]
        kv_ref[d, c] = jnp.concatenate(kv, axis=1)
        al_ref[d, c] = jnp.exp(b_last)


def _gla_prep(ym, la, lay):
    R = ym.shape[0]
    tm = lay.tm
    nc = tm // CHUNK
    wk = GLA_H * GLA_DK
    wv = GLA_H * GLA_DV
    cq, ck, cv = OFF_GQ // wk, OFF_GK // wk, OFF_GV // wv
    drow = lambda i: (0, i, 0)
    body = functools.partial(_gla_prep_body, nc=nc)
    return pl.pallas_call(
        body,
        out_shape=(jax.ShapeDtypeStruct((2, R, wv), bf16), jax.ShapeDtypeStruct((2, R, wk), bf16),
                   jax.ShapeDtypeStruct((2, R // CHUNK, GLA_DV, wk), f32),
                   jax.ShapeDtypeStruct((2, R // CHUNK, 1, wk), f32)),
        grid=(R // tm,),
        in_specs=[pl.BlockSpec((tm, wk), lambda i: (i, cq)), pl.BlockSpec((tm, wk), lambda i: (i, ck)),
                  pl.BlockSpec((tm, wv), lambda i: (i, cv)), pl.BlockSpec((2, tm, wk), drow)],
        out_specs=(pl.BlockSpec((2, tm, wv), drow), pl.BlockSpec((2, tm, wk), drow),
                   pl.BlockSpec((2, nc, GLA_DV, wk), lambda i: (0, i, 0, 0)),
                   pl.BlockSpec((2, nc, 1, wk), lambda i: (0, i, 0, 0))),
        compiler_params=_cparams("parallel"),
        name="gla_prep",
    )(ym, ym, ym, la)


def _recout_body(df_ref, db_ref, gf_ref, gb_ref, gi_ref, z_ref, r_ref, x_ref, m_ref, dn_ref, gn_ref, w_ref, o_ref):
    up = lambda r: r.astype(f32)
    dn = up(df_ref[...]) + up(db_ref[...])
    gl = (up(gf_ref[...]) + up(gi_ref[0])) + (up(gb_ref[...]) + up(gi_ref[1]))
    z = up(z_ref[...])
    r = up(r_ref[...])
    parts = []
    for src, gate, gain, hd, nh in ((dn, z, dn_ref[...], DN_DV, DN_H), (gl, r, gn_ref[...], GLA_DV, GLA_H)):
        for h in range(nh):
            a = src[:, h * hd:(h + 1) * hd]
            a = a * lax.rsqrt(jnp.mean(a * a, axis=-1, keepdims=True) + EPS) * gain
            parts.append(a * _silu(gate[:, h * hd:(h + 1) * hd]))
    mix = jnp.concatenate(parts, axis=1).astype(bf16)
    o_ref[...] = x_ref[...] + m_ref[0][2:3, :] * _dot(mix, w_ref[...])


def _recout(dn_f, dn_b, gl_f, gl_b, gl_i, ym, xs, mods_l, dn_norm, gla_norm, w, lay):
    R, D = xs.shape
    tm = lay.tmf
    wd = DN_H * DN_DV
    wg = GLA_H * GLA_DV
    row = lambda i: (i, 0)
    return pl.pallas_call(
        _recout_body,
        out_shape=jax.ShapeDtypeStruct((R, D), f32),
        grid=(R // tm,),
        in_specs=[pl.BlockSpec((tm, wd), row), pl.BlockSpec((tm, wd), row),
                  pl.BlockSpec((tm, wg), row), pl.BlockSpec((tm, wg), row),
                  pl.BlockSpec((2, tm, wg), lambda i: (0, i, 0)),
                  pl.BlockSpec((tm, wd), lambda i: (i, OFF_DZ // wd)),
                  pl.BlockSpec((tm, wg), lambda i: (i, OFF_GR // wg)),
                  pl.BlockSpec((tm, D), row),
                  pl.BlockSpec((1, 6, D), lambda i: (lay.mod_row(i * tm), 0, 0)),
                  _const_spec(dn_norm.shape), _const_spec(gla_norm.shape), _const_spec(w.shape)],
        out_specs=pl.BlockSpec((tm, D), row),
        compiler_params=_cparams("parallel"),
        name="rec_outproj",
    )(dn_f, dn_b, gl_f, gl_b, gl_i, ym, ym, xs, mods_l, dn_norm, gla_norm, w)


def _rec_weights(w_in):
    sizes = (DN_H * DN_DK, DN_H * DN_DK, DN_H * DN_DV, DN_H * DN_DV, 2 * DN_H, 2 * DN_H,
             GLA_H * GLA_DK, GLA_H * GLA_DK, GLA_H * GLA_DV, GLA_H * GLA_DV, 2 * GLA_RANK)
    offs = [0]
    for s in sizes:
        offs.append(offs[-1] + s)
    seg = [w_in[:, offs[i]:offs[i + 1]] for i in range(len(sizes))]
    dq, dk, dv, dz, da, db, gq, gk, gv, gr, gg = seg
    main = jnp.concatenate([dq, dk, dv, dz, gq, gk, gv, gr], axis=1).astype(bf16)
    small = jnp.concatenate([da, db, gg], axis=1)
    small = jnp.pad(small, ((0, 0), (0, LANES - small.shape[1]))).astype(bf16)
    return main, small


def _pad_lanes(v, off=0):
    v = v.reshape(1, -1)
    return jnp.pad(v, ((0, 0), (off, LANES - off - v.shape[1])))


def _gla_gate_weights(w2):
    out = jnp.zeros((2, LANES, w2.shape[2]), f32)
    for d in range(2):
        out = out.at[d, SM_GG + d * GLA_RANK:SM_GG + (d + 1) * GLA_RANK, :].set(w2[d])
    hi = out.astype(bf16)
    lo = (out - hi.astype(f32)).astype(bf16)
    return hi, lo


def _rope_tables(lay):
    T, Tc, B = lay.T, lay.Tc, lay.B
    t = jnp.arange(T, dtype=jnp.int32)
    row = (t // GRID_W).astype(f32)
    col = (t % GRID_W).astype(f32)
    inv_freq = ROPE_THETA ** (-jnp.arange(ROPE_PAIRS, dtype=f32) / ROPE_PAIRS)
    ar = row[:, None] * inv_freq
    ac = col[:, None] * inv_freq
    cos = jnp.concatenate([jnp.cos(ar), jnp.cos(ar), jnp.cos(ac), jnp.cos(ac)], axis=1)
    sin = jnp.concatenate([-jnp.sin(ar), jnp.sin(ar), -jnp.sin(ac), jnp.sin(ac)], axis=1)
    return cos, sin


def kernel(x, c, ctx, c_ctx, mod_w, mod_b, rec_w_in, rec_conv, dn_a_log, dn_dt_bias, dn_norm, gla_w2, gla_b2,
           gla_norm, rec_w_out, att_w_qkv, att_q_norm, att_k_norm, att_w_out, ffn_w_up, ffn_conv, ffn_w_down,
           final_norm):
    B, T, D = x.shape
    Tc = ctx.shape[1]
    depth = mod_w.shape[0]
    lay = _Layout(B, T, Tc)
    assert B + 1 <= MOD_ROWS
    xs = jnp.concatenate([ctx.reshape(B * Tc, D), x.reshape(B * T, D)], axis=0)
    cond = jnp.concatenate([c, c_ctx[None], jnp.zeros((MOD_ROWS - B - 1, D), f32)], axis=0)
    mods = _mods(cond, mod_w, mod_b)
    cos, sin = _rope_tables(lay)

    for i in range(depth):
        ml = mods[i]
        if i % 2 == 0:
            e = i // 2
            wm, ws = _rec_weights(rec_w_in[e])
            ym, ys = _recin(xs, ml, wm, ws, lay)
            w2h, w2l = _gla_gate_weights(gla_w2[e])
            q, k, v, gates, la = _recprep(
                ym, ys, rec_conv[e], _pad_lanes(dn_a_log[e], SM_DA), _pad_lanes(dn_dt_bias[e], SM_DA),
                w2h, w2l, gla_b2[e].reshape(2, 1, -1), lay)
            gl_i, qi, kv, al = _gla_prep(ym, la, lay)
            dn_f, dn_b, gl_f, gl_b = _rec_scan(_delta_prep(q, k, v, gates, lay), (qi, kv, al), lay)
            xs = _recout(dn_f, dn_b, gl_f, gl_b, gl_i, ym, xs, ml, dn_norm[e].reshape(1, -1),
                         gla_norm[e].reshape(1, -1), rec_w_out[e].astype(bf16), lay)
        else:
            o = i // 2
            q, k, v = _qkv(xs, ml, att_w_qkv[o].astype(bf16), att_q_norm[o].reshape(1, -1),
                           att_k_norm[o].reshape(1, -1), cos, sin, lay)
            a = _flash(q, k, v, lay)
            xs = _outproj(a, xs, ml, att_w_out[o].astype(bf16), lay)
        last = i == depth - 1
        xs = _ffn(xs, ml, ffn_w_up[i].astype(bf16), ffn_conv[i], ffn_w_down[i].astype(bf16), lay,
                  final_gain=final_norm.reshape(1, -1) if last else None)

    return xs.reshape(B, T, D)
```

```python
import functools
import math

import jax
import jax.numpy as jnp
from jax import lax
from jax.experimental import pallas as pl
from jax.experimental.pallas import tpu as pltpu

f32 = jnp.float32
bf16 = jnp.bfloat16

EPS = 1e-6
GRID_W = 64
DN_H = 4
DN_DK = 128
DN_DV = 128
GLA_H = 4
GLA_DK = 64
GLA_DV = 128
GLA_RANK = 16
GLA_TAU = 16.0
CHUNK = 64
ATT_H = 8
ATT_KVH = 2
ATT_G = ATT_H // ATT_KVH
ATT_HD = 128
ROPE_THETA = 10000.0
ROPE_PAIRS = ATT_HD // 4
LOG2E = math.log2(math.e)
FFN_CHUNK = 256
MOD_ROWS = 8
LANES = 128
VMEM_LIMIT = 56 * 1024 * 1024


def _cparams(*sem):
    return pltpu.CompilerParams(dimension_semantics=sem, vmem_limit_bytes=VMEM_LIMIT)


def _const_spec(shape):
    nd = len(shape)
    return pl.BlockSpec(shape, lambda *_: (0,) * nd, pipeline_mode=pl.Buffered(1))


class _Layout:
    def __init__(self, B, T, Tc):
        self.B, self.T, self.Tc = B, T, Tc
        self.R = B * Tc + B * T
        self.tm = min(256, Tc)
        assert Tc % self.tm == 0 and T % self.tm == 0 and T % CHUNK == 0 and Tc % CHUNK == 0
        self.tmf = min(512, B * Tc)
        assert (B * Tc) % self.tmf == 0 and T % self.tmf == 0
        self.seq_starts = [b * Tc for b in range(B)] + [B * Tc + b * T for b in range(B)]

    def mod_row(self, row0):
        return jnp.where(row0 < self.B * self.Tc, self.B, (row0 - self.B * self.Tc) // self.T)

    def edge_masks(self, rows):
        first = rows == self.seq_starts[0]
        last = rows == self.R - 1
        for s in self.seq_starts[1:]:
            first = jnp.logical_or(first, rows == s)
            last = jnp.logical_or(last, rows == s - 1)
        return first, last


def _normmod(x, m, shift_i, scale_i):
    ms = jnp.mean(x * x, axis=-1, keepdims=True)
    return (x * lax.rsqrt(ms + EPS)) * (1.0 + m[scale_i:scale_i + 1, :]) + m[shift_i:shift_i + 1, :]


def _silu(x):
    return x * jax.nn.sigmoid(x)


def _split2(x):
    hi = x.astype(bf16)
    lo = (x - hi.astype(f32)).astype(bf16)
    return hi, lo


def _dot(a, b):
    return jnp.dot(a, b, preferred_element_type=f32)


def _dot_nt(a, b):
    return lax.dot_general(a, b, (((1,), (1,)), ((), ())), preferred_element_type=f32)


def _dot_tn(a, b):
    return lax.dot_general(a, b, (((0,), (0,)), ((), ())), preferred_element_type=f32)


def _mod_body(c_ref, w_ref, b_ref, o_ref):
    a = _silu(c_ref[...]).astype(bf16)
    o_ref[0] = _dot(a, w_ref[0].astype(bf16)) + b_ref[0]


def _mods(cond, mod_w, mod_b):
    depth, D, D6 = mod_w.shape
    tn = 1536
    out = pl.pallas_call(
        _mod_body,
        out_shape=jax.ShapeDtypeStruct((depth, MOD_ROWS, D6), f32),
        grid=(depth, D6 // tn),
        in_specs=[pl.BlockSpec((MOD_ROWS, D), lambda l, j: (0, 0)),
                  pl.BlockSpec((1, D, tn), lambda l, j: (l, 0, j)),
                  pl.BlockSpec((1, 1, tn), lambda l, j: (l, 0, j))],
        out_specs=pl.BlockSpec((1, MOD_ROWS, tn), lambda l, j: (l, 0, j)),
        compiler_params=_cparams("parallel", "parallel"),
        name="mods",
    )(cond, mod_w, mod_b.reshape(depth, 1, D6))
    return out.reshape(depth, MOD_ROWS, 6, D)


def _ffn_body(xp_ref, x_ref, xn_ref, m_ref, wu_ref, cw_ref, wd_ref, *rest, lay, halo, off, final):
    fg_ref, o_ref, h_sc, g_sc, acc_sc = rest if final else (None,) + rest
    tm = lay.tmf
    F = wd_ref.shape[0]
    row0 = (pl.program_id(0) + off) * tm
    m = m_ref[0]
    x = x_ref[...]
    xe = jnp.concatenate([xp_ref[...], x, xn_ref[...]], axis=0)
    h_sc[...] = _normmod(xe, m, 3, 4).astype(bf16)
    kill_prev, kill_next = lay.edge_masks(row0 + lax.broadcasted_iota(jnp.int32, (tm, 1), 0))
    acc_sc[...] = jnp.zeros_like(acc_sc)

    for c0 in range(0, F, FFN_CHUNK):
        cs = slice(c0, c0 + FFN_CHUNK)
        g_sc[...] = _dot(h_sc[...], wu_ref[:, cs])
        val = _dot(h_sc[pl.ds(halo, tm), :], wu_ref[:, F + c0:F + c0 + FFN_CHUNK])
        gm1 = jnp.where(kill_prev, 0.0, g_sc[pl.ds(halo - 1, tm), :])
        g0 = g_sc[pl.ds(halo, tm), :]
        gp1 = jnp.where(kill_next, 0.0, g_sc[pl.ds(halo + 1, tm), :])
        gate = gm1 * cw_ref[0:1, cs] + g0 * cw_ref[1:2, cs] + gp1 * cw_ref[2:3, cs]
        act = (_silu(gate) * val).astype(bf16)
        acc_sc[...] += _dot(act, wd_ref[cs, :])
    y = x + m[5:6, :] * acc_sc[...]
    if final:
        y = y * lax.rsqrt(jnp.mean(y * y, axis=-1, keepdims=True) + EPS) * fg_ref[...]
    o_ref[...] = y


def _ffn(xs, mods_l, wu, cw, wd, lay, final_gain=None):
    R, D = xs.shape
    tm = lay.tmf
    halo = 16
    cf = FFN_CHUNK
    assert wd.shape[0] % cf == 0 and wu.shape[1] == 2 * wd.shape[0]
    nh = tm // halo
    nhb = R // halo
    final = final_gain is not None
    off = lay.B * lay.Tc // tm if final else 0
    body = functools.partial(_ffn_body, lay=lay, halo=halo, off=off, final=final)
    extra_specs = [_const_spec(final_gain.shape)] if final else []
    extra_args = [final_gain] if final else []
    return pl.pallas_call(
        body,
        out_shape=jax.ShapeDtypeStruct((R - off * tm, D), f32),
        grid=(R // tm - off,),
        in_specs=[pl.BlockSpec((halo, D), lambda i: (jnp.maximum((i + off) * nh - 1, 0), 0)),
                  pl.BlockSpec((tm, D), lambda i: (i + off, 0)),
                  pl.BlockSpec((halo, D), lambda i: (jnp.minimum((i + off + 1) * nh, nhb - 1), 0)),
                  pl.BlockSpec((1, 6, D), lambda i: (lay.mod_row((i + off) * tm), 0, 0)),
                  _const_spec(wu.shape), _const_spec(cw.shape), _const_spec(wd.shape)]
        + extra_specs,
        out_specs=pl.BlockSpec((tm, D), lambda i: (i, 0)),
        scratch_shapes=[pltpu.VMEM((tm + 2 * halo, D), bf16),
                        pltpu.VMEM((tm + 2 * halo, cf), f32),
                        pltpu.VMEM((tm, D), f32)],
        compiler_params=_cparams("parallel"),
        name="conv_ffn",
    )(xs, xs, xs, mods_l, wu, cw, wd, *extra_args)


def _qkv_body(x_ref, m_ref, w_ref, qn_ref, kn_ref, cos_ref, sin_ref, q_ref, k_ref, v_ref, *, lay):
    h = _normmod(x_ref[...], m_ref[0], 0, 1).astype(bf16)
    y = _dot(h, w_ref[...])
    is_ctx = pl.program_id(0) * lay.tmf < lay.B * lay.Tc
    cos = jnp.where(is_ctx, 1.0, cos_ref[...])
    sin = jnp.where(is_ctx, 0.0, sin_ref[...])
    lane = lax.broadcasted_iota(jnp.int32, cos.shape, 1)
    first = (lane % (2 * ROPE_PAIRS)) < ROPE_PAIRS

    def norm_rope(z, gain):
        z = z * lax.rsqrt(jnp.mean(z * z, axis=-1, keepdims=True) + EPS) * gain
        partner = jnp.where(first, pltpu.roll(z, LANES - ROPE_PAIRS, axis=1), pltpu.roll(z, ROPE_PAIRS, axis=1))
        return z * cos + partner * sin

    nq = ATT_H * ATT_HD
    for hh in range(ATT_H):
        z = norm_rope(y[:, hh * ATT_HD:(hh + 1) * ATT_HD], qn_ref[...])
        q_ref[:, hh * ATT_HD:(hh + 1) * ATT_HD] = (z * (ATT_HD ** -0.5 * LOG2E)).astype(bf16)
    for kk in range(ATT_KVH):
        z = norm_rope(y[:, nq + kk * ATT_HD:nq + (kk + 1) * ATT_HD], kn_ref[...])
        k_ref[:, kk * ATT_HD:(kk + 1) * ATT_HD] = z.astype(bf16)
    nk = nq + ATT_KVH * ATT_HD
    ones = jnp.ones((y.shape[0], ATT_HD), bf16)
    for kk in range(ATT_KVH):
        v_ref[:, 2 * kk * ATT_HD:(2 * kk + 1) * ATT_HD] = y[:, nk + kk * ATT_HD:nk + (kk + 1) * ATT_HD].astype(bf16)
        v_ref[:, (2 * kk + 1) * ATT_HD:(2 * kk + 2) * ATT_HD] = ones


def _qkv(xs, mods_l, w, qn, kn, cos, sin, lay):
    R, D = xs.shape
    tm = lay.tmf
    nq = ATT_H * ATT_HD
    nkv = ATT_KVH * ATT_HD
    nct, nlt = lay.B * lay.Tc // tm, lay.T // tm
    pos = lambda i: (jnp.where(i < nct, 0, (i - nct) % nlt), 0)
    kv_rows = lambda i: (jnp.where(i < nct, lay.B * nlt + i, i - nct), 0)
    return pl.pallas_call(
        functools.partial(_qkv_body, lay=lay),
        out_shape=(jax.ShapeDtypeStruct((R, nq), bf16), jax.ShapeDtypeStruct((R, nkv), bf16),
                   jax.ShapeDtypeStruct((R, 2 * nkv), bf16)),
        grid=(R // tm,),
        in_specs=[pl.BlockSpec((tm, D), lambda i: (i, 0)),
                  pl.BlockSpec((1, 6, D), lambda i: (lay.mod_row(i * tm), 0, 0)),
                  _const_spec(w.shape), _const_spec(qn.shape), _const_spec(kn.shape),
                  pl.BlockSpec((tm, ATT_HD), pos),
                  pl.BlockSpec((tm, ATT_HD), pos)],
        out_specs=(pl.BlockSpec((tm, nq), lambda i: (i, 0)), pl.BlockSpec((tm, nkv), kv_rows),
                   pl.BlockSpec((tm, 2 * nkv), kv_rows)),
        compiler_params=_cparams("parallel"),
        name="qkv_rope",
    )(xs, mods_l, w, qn, kn, cos, sin)


def _flash_body(q_ref, kc_ref, vc_ref, kl_ref, vl_ref, o_ref, m_sc, acc_sc, *, tq, ncq, ck):
    qi = pl.program_id(2)
    q = q_ref[...]
    q4 = jnp.concatenate([q[:, g * ATT_HD:(g + 1) * ATT_HD] for g in range(ATT_G)], axis=0)
    m_sc[...] = jnp.full_like(m_sc, -jnp.inf)
    acc_sc[...] = jnp.zeros_like(acc_sc)

    def chunk(s, v):
        m_prev = m_sc[...]
        m_new = jnp.maximum(m_prev, jnp.max(s, axis=-1, keepdims=True))
        p = jnp.exp2(s - jnp.tile(m_new, (1, s.shape[1] // LANES)))
        alpha = jnp.exp2(m_prev - m_new)
        acc_sc[...] = jnp.tile(alpha, (1, 2)) * acc_sc[...] + _dot(p.astype(bf16), v)
        m_sc[...] = m_new

    @pl.when(qi < ncq)
    def _():
        chunk(_dot_nt(q4, kc_ref[...]), vc_ref[...])

    @pl.when(qi >= ncq)
    def _():
        n = kl_ref.shape[0] // ck
        s_ctx = _dot_nt(q4, kc_ref[...])
        s_next = _dot_nt(q4, kl_ref[0:ck, :])
        chunk(s_ctx, vc_ref[...])
        for c in range(n):
            s = s_next
            if c + 1 < n:
                s_next = _dot_nt(q4, kl_ref[(c + 1) * ck:(c + 2) * ck, :])
            chunk(s, vl_ref[c * ck:(c + 1) * ck, :])

    acc = acc_sc[...]
    out = acc[:, :ATT_HD] / acc[:, ATT_HD:]
    for g in range(ATT_G):
        o_ref[:, g * ATT_HD:(g + 1) * ATT_HD] = out[g * tq:(g + 1) * tq, :].astype(bf16)


def _flash(q, k, v1, lay):
    R = q.shape[0]
    B, T, Tc = lay.B, lay.T, lay.Tc
    tq = lay.tm
    ncq, nlq = Tc // tq, T // tq
    gw = ATT_G * ATT_HD
    ck = min(2048, T)
    assert T % ck == 0
    assert (B * T) % Tc == 0

    def q_map(b, kv, qi):
        return (jnp.where(qi < ncq, b * ncq + qi, B * ncq + b * nlq + (qi - ncq)), kv)

    lat_map = lambda b, kv, qi: (b, kv)
    ctx_map = lambda b, kv, qi: (B * T // Tc + b, kv)
    body = functools.partial(_flash_body, tq=tq, ncq=ncq, ck=ck)
    return pl.pallas_call(
        body,
        out_shape=jax.ShapeDtypeStruct((R, ATT_H * ATT_HD), bf16),
        grid=(B, ATT_KVH, ncq + nlq),
        in_specs=[pl.BlockSpec((tq, gw), q_map),
                  pl.BlockSpec((Tc, ATT_HD), ctx_map), pl.BlockSpec((Tc, 2 * ATT_HD), ctx_map),
                  pl.BlockSpec((T, ATT_HD), lat_map), pl.BlockSpec((T, 2 * ATT_HD), lat_map)],
        out_specs=pl.BlockSpec((tq, gw), q_map),
        scratch_shapes=[pltpu.VMEM((ATT_G * tq, LANES), f32), pltpu.VMEM((ATT_G * tq, 2 * ATT_HD), f32)],
        compiler_params=_cparams("parallel", "parallel", "parallel"),
        name="flash_gqa",
    )(q, k, v1, k, v1)


def _outproj_body(a_ref, x_ref, m_ref, w_ref, o_ref):
    o_ref[...] = x_ref[...] + m_ref[0][2:3, :] * _dot(a_ref[...], w_ref[...])


def _outproj(a, xs, mods_l, w, lay):
    R, D = xs.shape
    tm = lay.tmf
    K = a.shape[1]
    return pl.pallas_call(
        _outproj_body,
        out_shape=jax.ShapeDtypeStruct((R, D), f32),
        grid=(R // tm,),
        in_specs=[pl.BlockSpec((tm, K), lambda i: (i, 0)),
                  pl.BlockSpec((tm, D), lambda i: (i, 0)),
                  pl.BlockSpec((1, 6, D), lambda i: (lay.mod_row(i * tm), 0, 0)),
                  _const_spec(w.shape)],
        out_specs=pl.BlockSpec((tm, D), lambda i: (i, 0)),
        compiler_params=_cparams("parallel"),
        name="att_outproj",
    )(a, xs, mods_l, w)


REC_MAIN = 2 * DN_H * DN_DK + 2 * DN_H * DN_DV + 2 * GLA_H * GLA_DK + 2 * GLA_H * GLA_DV
REC_QKV = 2 * DN_H * DN_DK + DN_H * DN_DV
OFF_DZ = REC_QKV
OFF_GQ = OFF_DZ + DN_H * DN_DV
OFF_GK = OFF_GQ + GLA_H * GLA_DK
OFF_GV = OFF_GK + GLA_H * GLA_DK
OFF_GR = OFF_GV + GLA_H * GLA_DV
SM_DA = 0
SM_DB = 2 * DN_H
SM_GG = 4 * DN_H


def _recin_body(x_ref, m_ref, wm_ref, ws_ref, ym_ref, ys_ref):
    h = _normmod(x_ref[...], m_ref[0], 0, 1).astype(bf16)
    ym_ref[...] = _dot(h, wm_ref[...]).astype(bf16)
    ys_ref[...] = _dot(h, ws_ref[...])


def _recin(xs, mods_l, wm, ws, lay):
    R, D = xs.shape
    tm = lay.tmf
    return pl.pallas_call(
        _recin_body,
        out_shape=(jax.ShapeDtypeStruct((R, REC_MAIN), bf16), jax.ShapeDtypeStruct((R, LANES), f32)),
        grid=(R // tm,),
        in_specs=[pl.BlockSpec((tm, D), lambda i: (i, 0)),
                  pl.BlockSpec((1, 6, D), lambda i: (lay.mod_row(i * tm), 0, 0)),
                  _const_spec(wm.shape), _const_spec(ws.shape)],
        out_specs=(pl.BlockSpec((tm, REC_MAIN), lambda i: (i, 0)), pl.BlockSpec((tm, LANES), lambda i: (i, 0))),
        compiler_params=_cparams("parallel"),
        name="rec_inproj",
    )(xs, mods_l, wm, ws)


def _recprep_body(yp_ref, y_ref, yn_ref, ys_ref, cw_ref, av_ref, dtb_ref, w2h_ref, w2l_ref, b2_ref,
                  q_ref, k_ref, v_ref, g_ref, la_ref, e_sc, *, lay, halo):
    tm = lay.tm
    row0 = pl.program_id(0) * tm
    e_sc[0:halo, :] = yp_ref[...].astype(f32)
    e_sc[halo:halo + tm, :] = y_ref[...].astype(f32)
    e_sc[halo + tm:halo + tm + halo, :] = yn_ref[...].astype(f32)
    kill_prev, kill_next = lay.edge_masks(row0 + lax.broadcasted_iota(jnp.int32, (tm, 1), 0))
    cw = cw_ref[...]
    gm1 = jnp.where(kill_prev, 0.0, e_sc[pl.ds(halo - 1, tm), :])
    g0 = e_sc[pl.ds(halo, tm), :]
    gp1 = jnp.where(kill_next, 0.0, e_sc[pl.ds(halo + 1, tm), :])
    s = _silu(gm1 * cw[0:1, :] + g0 * cw[1:2, :] + gp1 * cw[2:3, :])
    nqk = DN_H * DN_DK
    for h in range(DN_H):
        z = s[:, h * DN_DK:(h + 1) * DN_DK]
        qn = z * lax.rsqrt(jnp.sum(z * z, axis=-1, keepdims=True) + EPS) * (DN_DK ** -0.5)
        q_ref[:, h * DN_DK:(h + 1) * DN_DK] = qn.astype(bf16)
        z = s[:, nqk + h * DN_DK:nqk + (h + 1) * DN_DK]
        k_ref[:, h * DN_DK:(h + 1) * DN_DK] = (z * lax.rsqrt(jnp.sum(z * z, axis=-1, keepdims=True) + EPS)).astype(bf16)
    v_ref[...] = s[:, 2 * nqk:].astype(bf16)
    z = ys_ref[...]
    lane = lax.broadcasted_iota(jnp.int32, z.shape, 1)
    g = -jnp.exp(av_ref[...]) * jax.nn.softplus(z + dtb_ref[...])
    beta = jax.nn.sigmoid(z)
    g_ref[...] = jnp.where(lane < SM_DB, g, jnp.where(lane < SM_GG, beta, 0.0))
    zh, zl = _split2(z)
    for d in range(2):
        la = _dot(zh, w2h_ref[d]) + _dot(zl, w2h_ref[d]) + _dot(zh, w2l_ref[d]) + b2_ref[d]
        la_ref[d] = jax.nn.log_sigmoid(la) / GLA_TAU


def _recprep(ym, ys, cw, av, dtb, w2h, w2l, b2, lay):
    R = ym.shape[0]
    tm = lay.tm
    halo = 16
    nh = tm // halo
    nhb = R // halo
    nd = DN_H * DN_DK
    body = functools.partial(_recprep_body, lay=lay, halo=halo)
    return pl.pallas_call(
        body,
        out_shape=(jax.ShapeDtypeStruct((R, nd), bf16), jax.ShapeDtypeStruct((R, nd), bf16),
                   jax.ShapeDtypeStruct((R, DN_H * DN_DV), bf16), jax.ShapeDtypeStruct((R, LANES), f32),
                   jax.ShapeDtypeStruct((2, R, GLA_H * GLA_DK), f32)),
        grid=(R // tm,),
        in_specs=[pl.BlockSpec((halo, REC_QKV), lambda i: (jnp.maximum(i * nh - 1, 0), 0)),
                  pl.BlockSpec((tm, REC_QKV), lambda i: (i, 0)),
                  pl.BlockSpec((halo, REC_QKV), lambda i: (jnp.minimum((i + 1) * nh, nhb - 1), 0)),
                  pl.BlockSpec((tm, LANES), lambda i: (i, 0)),
                  _const_spec(cw.shape), _const_spec(av.shape), _const_spec(dtb.shape),
                  _const_spec(w2h.shape), _const_spec(w2l.shape), _const_spec(b2.shape)],
        out_specs=(pl.BlockSpec((tm, nd), lambda i: (i, 0)), pl.BlockSpec((tm, nd), lambda i: (i, 0)),
                   pl.BlockSpec((tm, DN_H * DN_DV), lambda i: (i, 0)), pl.BlockSpec((tm, LANES), lambda i: (i, 0)),
                   pl.BlockSpec((2, tm, GLA_H * GLA_DK), lambda i: (0, i, 0))),
        scratch_shapes=[pltpu.VMEM((tm + 2 * halo, REC_QKV), f32)],
        compiler_params=_cparams("parallel"),
        name="rec_prep",
    )(ym, ym, ym, ys, cw, av, dtb, w2h, w2l, b2)


def _tri_masks(rev, width):
    ri = lax.broadcasted_iota(jnp.int32, (CHUNK, width), 0)
    ci = lax.broadcasted_iota(jnp.int32, (CHUNK, width), 1) % CHUNK
    if rev:
        return ri <= ci, ri < ci, ri >= ci
    return ri >= ci, ri > ci, ri <= ci


def _block_diag(x, nblk):
    w = x.shape[1]
    t = jnp.concatenate([x] * nblk, axis=0)
    rb = lax.broadcasted_iota(jnp.int32, t.shape, 0) // CHUNK
    cb = lax.broadcasted_iota(jnp.int32, t.shape, 1) // (w // nblk)
    return jnp.where(rb == cb, t, jnp.zeros_like(t))


def _gate_cols(gt, d, width):
    H = DN_H
    g = [jnp.broadcast_to(gt[:, SM_DA + d * H + h:SM_DA + d * H + h + 1], (CHUNK, width)) for h in range(H)]
    b = [jnp.broadcast_to(gt[:, SM_DB + d * H + h:SM_DB + d * H + h + 1], (CHUNK, width)) for h in range(H)]
    return jnp.concatenate(g, axis=1), jnp.concatenate(b, axis=1)


def _delta_prep_body(q_ref, k_ref, v_ref, g_ref, w_ref, u_ref, qd_ref, kd_ref, in_ref, el_ref, *, nc):
    H = DN_H
    insts = [(c, d) for c in range(nc) for d in range(2)]
    rows = [slice(c * CHUNK, (c + 1) * CHUNK) for c in range(nc)]
    masks = [_tri_masks(d == 1, H * CHUNK) for d in range(2)]
    mbs = [_tri_masks(d == 1, CHUNK)[0].astype(bf16) for d in range(2)]
    neg1 = jnp.full((CHUNK, CHUNK), -1.0, bf16)
    ri = lax.broadcasted_iota(jnp.int32, (CHUNK, H * CHUNK), 0)
    ci = lax.broadcasted_iota(jnp.int32, (CHUNK, H * CHUNK), 1) % CHUNK

    kkqk = []
    for c in range(nc):
        k_bf = k_ref[rows[c], :]
        kq = jnp.concatenate([k_bf, q_ref[rows[c], :]], axis=0)
        kkqk.append(_dot_nt(kq, _block_diag(k_bf, H)))

    G, diff = {}, {}
    for c, d in insts:
        gt = g_ref[rows[c], :]
        incl_t4 = masks[d][2]
        g128, _ = _gate_cols(gt, d, DN_DK)
        g64, _ = _gate_cols(gt, d, CHUNK)
        hi, lo = _split2(g128)
        G[c, d] = _dot(jnp.concatenate([mbs[d], mbs[d]], axis=1), jnp.concatenate([hi, lo], axis=0))
        hi6, lo6 = _split2(g64)
        zero = jnp.zeros_like(hi6)
        rhs = jnp.concatenate([hi6, lo6, jnp.where(incl_t4, hi6, zero), jnp.where(incl_t4, lo6, zero)], axis=0)
        diff[c, d] = _dot(jnp.concatenate([mbs[d], mbs[d], neg1, neg1], axis=1), rhs)

    a, p = {}, {}
    for c, d in insts:
        incl4, strict4, _ = masks[d]
        decay = jnp.where(incl4, jnp.exp(diff[c, d]), 0.0)
        _, b64 = _gate_cols(g_ref[rows[c], :], d, CHUNK)
        a[c, d] = jnp.where(strict4, b64 * kkqk[c][:CHUNK] * decay, 0.0)
        in_ref[d, rows[c], :] = (kkqk[c][CHUNK:] * decay).astype(bf16)
        p[c, d] = jnp.where(ri == ci, 1.0, 0.0) - jnp.where(ri // 2 == ci // 2, a[c, d], 0.0)

    m = 2
    while m < CHUNK:
        sib = jnp.logical_and(ri // (2 * m) == ci // (2 * m), ri // m != ci // m)
        x = {}
        for i in insts:
            x[i] = _dot(p[i].astype(bf16), _block_diag(jnp.where(sib, a[i], 0.0).astype(bf16), H))
        for i in insts:
            p[i] = p[i] - _dot(x[i].astype(bf16), _block_diag(p[i].astype(bf16), H))
        m *= 2

    for c, d in insts:
        rs = rows[c]
        Gc = G[c, d]
        tb = p[c, d].astype(bf16)
        last = 0 if d else CHUNK - 1
        g_last = Gc[last:last + 1, :]
        e_g = jnp.exp(Gc)
        _, b128 = _gate_cols(g_ref[rs, :], d, DN_DK)
        k_all = k_ref[rs, :]
        vbeta = v_ref[rs, :] * b128
        kbg = k_all * b128 * e_g
        qd_ref[d, rs, :] = (q_ref[rs, :] * e_g).astype(bf16)
        kd_ref[d, rs, :] = (k_all * jnp.exp(g_last - Gc)).astype(bf16)
        el_ref[d, c] = jnp.exp(g_last)
        for h in range(H):
            sl = slice(h * DN_DK, (h + 1) * DN_DK)
            sc = slice(h * CHUNK, (h + 1) * CHUNK)
            uw = _dot(tb[:, sc], jnp.concatenate([vbeta[:, sl], kbg[:, sl]], axis=1).astype(bf16))
            u_ref[d, rs, sl] = uw[:, :DN_DV]
            w_ref[d, rs, sl] = uw[:, DN_DV:].astype(bf16)


def _delta_prep(q, k, v, gates, lay):
    R = q.shape[0]
    tm = lay.tm
    nc = tm // CHUNK
    wq = DN_H * DN_DK
    wv = DN_H * DN_DV
    wi = DN_H * CHUNK
    row = lambda i: (i, 0)
    drow = lambda i: (0, i, 0)
    body = functools.partial(_delta_prep_body, nc=nc)
    return pl.pallas_call(
        body,
        out_shape=(jax.ShapeDtypeStruct((2, R, wq), bf16), jax.ShapeDtypeStruct((2, R, wv), f32),
                   jax.ShapeDtypeStruct((2, R, wq), bf16), jax.ShapeDtypeStruct((2, R, wq), bf16),
                   jax.ShapeDtypeStruct((2, R, wi), bf16), jax.ShapeDtypeStruct((2, R // CHUNK, 1, wq), f32)),
        grid=(R // tm,),
        in_specs=[pl.BlockSpec((tm, wq), row), pl.BlockSpec((tm, wq), row), pl.BlockSpec((tm, wv), row),
                  pl.BlockSpec((tm, LANES), row)],
        out_specs=(pl.BlockSpec((2, tm, wq), drow), pl.BlockSpec((2, tm, wv), drow),
                   pl.BlockSpec((2, tm, wq), drow), pl.BlockSpec((2, tm, wq), drow),
                   pl.BlockSpec((2, tm, wi), drow), pl.BlockSpec((2, nc, 1, wq), lambda i: (0, i, 0, 0))),
        compiler_params=_cparams("parallel"),
        name="delta_prep",
    )(q, k, v, gates)


def _rec_scan_body(wf, uf, qdf, kdf, inf, elf, wb, ub, qdb, kdb, inb, elb, qif, kvf, alf, qib, kvb, alb,
                   of_ref, ob_ref, gof_ref, gob_ref, s_sc, st_sc, *, g):
    @pl.when(pl.program_id(1) == 0)
    def _():
        s_sc[...] = jnp.zeros_like(s_sc)
        st_sc[...] = jnp.zeros_like(st_sc)

    gviews = ((qif, kvf, alf, gof_ref), (qib, kvb, alb, gob_ref))
    st = [st_sc[0], st_sc[1]]
    gstate = {}
    for j in range(g):
        for d in range(2):
            c = g - 1 - j if d else j
            gstate[d, c] = st[d].astype(bf16)
            st[d] = st[d] * gviews[d][2][0, c] + gviews[d][1][0, c]
    st_sc[0] = st[0]
    st_sc[1] = st[1]

    def gla_outputs(c):
        for d in range(2):
            qi_r, o_r = gviews[d][0], gviews[d][3]
            rs = slice(c * CHUNK, (c + 1) * CHUNK)
            for h in range(GLA_H):
                sk = slice(h * GLA_DK, (h + 1) * GLA_DK)
                o_r[rs, h * GLA_DV:(h + 1) * GLA_DV] = _dot_nt(qi_r[0, rs, sk], gstate[d, c][:, sk]).astype(bf16)

    views = ((wf, uf, qdf, kdf, inf, elf, of_ref), (wb, ub, qdb, kdb, inb, elb, ob_ref))
    chains = [(d, h) for d in range(2) for h in range(DN_H)]
    S = {ch: s_sc[ch[0], ch[1]] for ch in chains}
    for j in range(g):
        r1, vnb = {}, {}
        for d, h in chains:
            w_r, u_r, qd_r = views[d][0], views[d][1], views[d][2]
            rs = slice((g - 1 - j if d else j) * CHUNK, (g - j if d else j + 1) * CHUNK)
            sl = slice(h * DN_DK, (h + 1) * DN_DK)
            r1[d, h] = _dot(jnp.concatenate([w_r[0, rs, sl], qd_r[0, rs, sl]], axis=0), S[d, h].astype(bf16))
        gla_outputs(j)
        for d, h in chains:
            u_r = views[d][1]
            rs = slice((g - 1 - j if d else j) * CHUNK, (g - j if d else j + 1) * CHUNK)
            sl = slice(h * DN_DK, (h + 1) * DN_DK)
            vnb[d, h] = (u_r[0, rs, sl] - r1[d, h][:CHUNK]).astype(bf16)
        for d, h in chains:
            kd_r, in_r, el_r, o_r = views[d][3], views[d][4], views[d][5], views[d][6]
            c = g - 1 - j if d else j
            rs = slice(c * CHUNK, (c + 1) * CHUNK)
            sl = slice(h * DN_DK, (h + 1) * DN_DK)
            sc = slice(h * CHUNK, (h + 1) * CHUNK)
            o_r[rs, sl] = (r1[d, h][CHUNK:] + _dot(in_r[0, rs, sc], vnb[d, h])).astype(bf16)
            S[d, h] = S[d, h] * el_r[0, c][:, sl] + _dot_tn(kd_r[0, rs, sl], vnb[d, h])
    for d, h in chains:
        s_sc[d, h] = S[d, h]


def _group_maps(lay, g):
    B = lay.B
    ngc, ngl = lay.Tc // (CHUNK * g), lay.T // (CHUNK * g)

    def fwd(b, n):
        return jnp.where(n < ngc, b * ngc + n, B * ngc + b * ngl + (n - ngc))

    def bwd(b, n):
        return jnp.where(n < ngc, b * ngc + (ngc - 1 - n), B * ngc + b * ngl + (ngl - 1 - (n - ngc)))

    return fwd, bwd, ngc + ngl


def _scan_group(lay):
    g = min(4, lay.Tc // CHUNK)
    assert (lay.Tc // CHUNK) % g == 0 and (lay.T // CHUNK) % g == 0
    return g


def _rec_scan(delta_in, gla_in, lay):
    w, u, qd, kd, intra, el = delta_in
    qi, kv, al = gla_in
    R = w.shape[1]
    g = _scan_group(lay)
    fwd, bwd, nstep = _group_maps(lay, g)
    wq = DN_H * DN_DK
    wv = DN_H * DN_DV
    wi = DN_H * CHUNK
    wk = GLA_H * GLA_DK
    rows = g * CHUNK

    def delta_view(m, d):
        sp = lambda width: pl.BlockSpec((1, rows, width), lambda b, n: (d, m(b, n), 0))
        return [sp(wq), sp(wv), sp(wq), sp(wq), sp(wi), pl.BlockSpec((1, g, 1, wq), lambda b, n: (d, m(b, n), 0, 0))]

    def gla_view(m, d):
        return [pl.BlockSpec((1, rows, wk), lambda b, n: (d, m(b, n), 0)),
                pl.BlockSpec((1, g, GLA_DV, wk), lambda b, n: (d, m(b, n), 0, 0)),
                pl.BlockSpec((1, g, 1, wk), lambda b, n: (d, m(b, n), 0, 0))]

    def out(m, width):
        return pl.BlockSpec((rows, width), lambda b, n: (m(b, n), 0))

    o_shape = jax.ShapeDtypeStruct((R, wv), bf16)
    go_shape = jax.ShapeDtypeStruct((R, GLA_H * GLA_DV), bf16)
    body = functools.partial(_rec_scan_body, g=g)
    return pl.pallas_call(
        body,
        out_shape=(o_shape, o_shape, go_shape, go_shape),
        grid=(lay.B, nstep),
        in_specs=delta_view(fwd, 0) + delta_view(bwd, 1) + gla_view(fwd, 0) + gla_view(bwd, 1),
        out_specs=(out(fwd, wv), out(bwd, wv), out(fwd, GLA_H * GLA_DV), out(bwd, GLA_H * GLA_DV)),
        scratch_shapes=[pltpu.VMEM((2, DN_H, DN_DK, DN_DV), f32), pltpu.VMEM((2, GLA_DV, wk), f32)],
        compiler_params=_cparams("parallel", "arbitrary"),
        name="rec_scan",
    )(w, u, qd, kd, intra, el, w, u, qd, kd, intra, el, qi, kv, al, qi, kv, al)


def _gla_prep_body(q_ref, k_ref, v_ref, la_ref, oi_ref, qi_ref, kv_ref, al_ref, *, nc):
    H = GLA_H
    insts = [(c, d) for c in range(nc) for d in range(2)]
    rows = [slice(c * CHUNK, (c + 1) * CHUNK) for c in range(nc)]
    incl4 = [_tri_masks(d == 1, H * CHUNK)[0] for d in range(2)]
    mbs = [_tri_masks(d == 1, CHUNK)[0].astype(bf16) for d in range(2)]
    b = {}
    for c, d in insts:
        hi, lo = _split2(la_ref[d, rows[c], :])
        b[c, d] = _dot(jnp.concatenate([mbs[d], mbs[d]], axis=1), jnp.concatenate([hi, lo], axis=0))
    att = {}
    for c, d in insts:
        q = q_ref[rows[c], :].astype(f32) * (GLA_DK ** -0.5)
        mid = CHUNK - 1 - CHUNK // 2 if d else CHUNK // 2
        b_mid = b[c, d][mid:mid + 1, :]
        qe = (q * jnp.exp(b[c, d] - b_mid)).astype(bf16)
        ke = (k_ref[rows[c], :] * jnp.exp(b_mid - b[c, d])).astype(bf16)
        att[c, d] = jnp.where(incl4[d], _dot_nt(qe, _block_diag(ke, H)), 0.0).astype(bf16)
        qi_ref[d, rows[c], :] = (q * jnp.exp(b[c, d])).astype(bf16)
    for c, d in insts:
        v_bf = v_ref[rows[c], :]
        oi_ref[d, rows[c], :] = _dot(att[c, d], _block_diag(v_bf, H)).astype(bf16)
        last = 0 if d else CHUNK - 1
        b_last = b[c, d][last:last + 1, :]
        ks = (k_ref[rows[c], :] * jnp.exp(b_last - b[c, d])).astype(bf16)
        kv = [_dot_tn(v_bf[:, h * GLA_DV:(h + 1) * GLA_DV], ks[:, h * GLA_DK:(h + 1) * GLA_DK]) for h in range(H)]
        kv_ref[d, c] = jnp.concatenate(kv, axis=1)
        al_ref[d, c] = jnp.exp(b_last)


def _gla_prep(ym, la, lay):
    R = ym.shape[0]
    tm = lay.tm
    nc = tm // CHUNK
    wk = GLA_H * GLA_DK
    wv = GLA_H * GLA_DV
    cq, ck, cv = OFF_GQ // wk, OFF_GK // wk, OFF_GV // wv
    drow = lambda i: (0, i, 0)
    body = functools.partial(_gla_prep_body, nc=nc)
    return pl.pallas_call(
        body,
        out_shape=(jax.ShapeDtypeStruct((2, R, wv), bf16), jax.ShapeDtypeStruct((2, R, wk), bf16),
                   jax.ShapeDtypeStruct((2, R // CHUNK, GLA_DV, wk), f32),
                   jax.ShapeDtypeStruct((2, R // CHUNK, 1, wk), f32)),
        grid=(R // tm,),
        in_specs=[pl.BlockSpec((tm, wk), lambda i: (i, cq)), pl.BlockSpec((tm, wk), lambda i: (i, ck)),
                  pl.BlockSpec((tm, wv), lambda i: (i, cv)), pl.BlockSpec((2, tm, wk), drow)],
        out_specs=(pl.BlockSpec((2, tm, wv), drow), pl.BlockSpec((2, tm, wk), drow),
                   pl.BlockSpec((2, nc, GLA_DV, wk), lambda i: (0, i, 0, 0)),
                   pl.BlockSpec((2, nc, 1, wk), lambda i: (0, i, 0, 0))),
        compiler_params=_cparams("parallel"),
        name="gla_prep",
    )(ym, ym, ym, la)


def _recout_body(df_ref, db_ref, gf_ref, gb_ref, gi_ref, z_ref, r_ref, x_ref, m_ref, dn_ref, gn_ref, w_ref, o_ref):
    up = lambda r: r.astype(f32)
    dn = up(df_ref[...]) + up(db_ref[...])
    gl = (up(gf_ref[...]) + up(gi_ref[0])) + (up(gb_ref[...]) + up(gi_ref[1]))
    z = up(z_ref[...])
    r = up(r_ref[...])
    parts = []
    for src, gate, gain, hd, nh in ((dn, z, dn_ref[...], DN_DV, DN_H), (gl, r, gn_ref[...], GLA_DV, GLA_H)):
        for h in range(nh):
            a = src[:, h * hd:(h + 1) * hd]
            a = a * lax.rsqrt(jnp.mean(a * a, axis=-1, keepdims=True) + EPS) * gain
            parts.append(a * _silu(gate[:, h * hd:(h + 1) * hd]))
    mix = jnp.concatenate(parts, axis=1).astype(bf16)
    o_ref[...] = x_ref[...] + m_ref[0][2:3, :] * _dot(mix, w_ref[...])


def _recout(dn_f, dn_b, gl_f, gl_b, gl_i, ym, xs, mods_l, dn_norm, gla_norm, w, lay):
    R, D = xs.shape
    tm = lay.tmf
    wd = DN_H * DN_DV
    wg = GLA_H * GLA_DV
    row = lambda i: (i, 0)
    return pl.pallas_call(
        _recout_body,
        out_shape=jax.ShapeDtypeStruct((R, D), f32),
        grid=(R // tm,),
        in_specs=[pl.BlockSpec((tm, wd), row), pl.BlockSpec((tm, wd), row),
                  pl.BlockSpec((tm, wg), row), pl.BlockSpec((tm, wg), row),
                  pl.BlockSpec((2, tm, wg), lambda i: (0, i, 0)),
                  pl.BlockSpec((tm, wd), lambda i: (i, OFF_DZ // wd)),
                  pl.BlockSpec((tm, wg), lambda i: (i, OFF_GR // wg)),
                  pl.BlockSpec((tm, D), row),
                  pl.BlockSpec((1, 6, D), lambda i: (lay.mod_row(i * tm), 0, 0)),
                  _const_spec(dn_norm.shape), _const_spec(gla_norm.shape), _const_spec(w.shape)],
        out_specs=pl.BlockSpec((tm, D), row),
        compiler_params=_cparams("parallel"),
        name="rec_outproj",
    )(dn_f, dn_b, gl_f, gl_b, gl_i, ym, ym, xs, mods_l, dn_norm, gla_norm, w)


def _rec_weights(w_in):
    sizes = (DN_H * DN_DK, DN_H * DN_DK, DN_H * DN_DV, DN_H * DN_DV, 2 * DN_H, 2 * DN_H,
             GLA_H * GLA_DK, GLA_H * GLA_DK, GLA_H * GLA_DV, GLA_H * GLA_DV, 2 * GLA_RANK)
    offs = [0]
    for s in sizes:
        offs.append(offs[-1] + s)
    seg = [w_in[:, offs[i]:offs[i + 1]] for i in range(len(sizes))]
    dq, dk, dv, dz, da, db, gq, gk, gv, gr, gg = seg
    main = jnp.concatenate([dq, dk, dv, dz, gq, gk, gv, gr], axis=1).astype(bf16)
    small = jnp.concatenate([da, db, gg], axis=1)
    small = jnp.pad(small, ((0, 0), (0, LANES - small.shape[1]))).astype(bf16)
    return main, small


def _pad_lanes(v, off=0):
    v = v.reshape(1, -1)
    return jnp.pad(v, ((0, 0), (off, LANES - off - v.shape[1])))


def _gla_gate_weights(w2):
    out = jnp.zeros((2, LANES, w2.shape[2]), f32)
    for d in range(2):
        out = out.at[d, SM_GG + d * GLA_RANK:SM_GG + (d + 1) * GLA_RANK, :].set(w2[d])
    hi = out.astype(bf16)
    lo = (out - hi.astype(f32)).astype(bf16)
    return hi, lo


def _rope_tables(lay):
    T, Tc, B = lay.T, lay.Tc, lay.B
    t = jnp.arange(T, dtype=jnp.int32)
    row = (t // GRID_W).astype(f32)
    col = (t % GRID_W).astype(f32)
    inv_freq = ROPE_THETA ** (-jnp.arange(ROPE_PAIRS, dtype=f32) / ROPE_PAIRS)
    ar = row[:, None] * inv_freq
    ac = col[:, None] * inv_freq
    cos = jnp.concatenate([jnp.cos(ar), jnp.cos(ar), jnp.cos(ac), jnp.cos(ac)], axis=1)
    sin = jnp.concatenate([-jnp.sin(ar), jnp.sin(ar), -jnp.sin(ac), jnp.sin(ac)], axis=1)
    return cos, sin


def kernel(x, c, ctx, c_ctx, mod_w, mod_b, rec_w_in, rec_conv, dn_a_log, dn_dt_bias, dn_norm, gla_w2, gla_b2,
           gla_norm, rec_w_out, att_w_qkv, att_q_norm, att_k_norm, att_w_out, ffn_w_up, ffn_conv, ffn_w_down,
           final_norm):
    B, T, D = x.shape
    Tc = ctx.shape[1]
    depth = mod_w.shape[0]
    lay = _Layout(B, T, Tc)
    assert B + 1 <= MOD_ROWS
    xs = jnp.concatenate([ctx.reshape(B * Tc, D), x.reshape(B * T, D)], axis=0)
    cond = jnp.concatenate([c, c_ctx[None], jnp.zeros((MOD_ROWS - B - 1, D), f32)], axis=0)
    mods = _mods(cond, mod_w, mod_b)
    cos, sin = _rope_tables(lay)

    for i in range(depth):
        ml = mods[i]
        if i % 2 == 0:
            e = i // 2
            wm, ws = _rec_weights(rec_w_in[e])
            ym, ys = _recin(xs, ml, wm, ws, lay)
            w2h, w2l = _gla_gate_weights(gla_w2[e])
            q, k, v, gates, la = _recprep(
                ym, ys, rec_conv[e], _pad_lanes(dn_a_log[e], SM_DA), _pad_lanes(dn_dt_bias[e], SM_DA),
                w2h, w2l, gla_b2[e].reshape(2, 1, -1), lay)
            gl_i, qi, kv, al = _gla_prep(ym, la, lay)
            dn_f, dn_b, gl_f, gl_b = _rec_scan(_delta_prep(q, k, v, gates, lay), (qi, kv, al), lay)
            xs = _recout(dn_f, dn_b, gl_f, gl_b, gl_i, ym, xs, ml, dn_norm[e].reshape(1, -1),
                         gla_norm[e].reshape(1, -1), rec_w_out[e].astype(bf16), lay)
        else:
            o = i // 2
            q, k, v = _qkv(xs, ml, att_w_qkv[o].astype(bf16), att_q_norm[o].reshape(1, -1),
                           att_k_norm[o].reshape(1, -1), cos, sin, lay)
            a = _flash(q, k, v, lay)
            xs = _outproj(a, xs, ml, att_w_out[o].astype(bf16), lay)
        last = i == depth - 1
        xs = _ffn(xs, ml, ffn_w_up[i].astype(bf16), ffn_conv[i], ffn_w_down[i].astype(bf16), lay,
                  final_gain=final_norm.reshape(1, -1) if last else None)

    return xs.reshape(B, T, D)
```
